```python
import math
import jax, jax.numpy as jnp
from jax import lax
import numpy as np

D_MODEL = 1024
BATCH = 2
SEQ = 8192
DEPTH = 4

A_QK_DIM = 64
A_V_DIM = 2 * A_QK_DIM
A_WIDTH = D_MODEL // 2
A_HEADS = A_WIDTH // A_V_DIM
Q_BLOCK = 128
ROPE_THETA = 10000.0
POOL_WINDOWS = (2, 4, 8, 16)
B_WIDTH = D_MODEL // 2
POOL_GROUP = B_WIDTH // len(POOL_WINDOWS)
AB_IN = 3 * A_WIDTH + B_WIDTH
C_DIM = 128
C_HEADS = D_MODEL // C_DIM
C_IN = 4 * D_MODEL
CHUNK = 64
D_FF = ((8 * D_MODEL // 3 + 255) // 256) * 256
ALPHA = (2 * DEPTH) ** 0.25
BETA = (8 * DEPTH) ** -0.25
LN_EPS = 1e-5
N_AB = (DEPTH + 1) // 2
N_C = DEPTH // 2

kernel_name = "diffattn_pool_hgrn2_macaron_deepnorm"


def layer_norm(x, g, b):
    xf = x.astype(jnp.float32)
    mu = jnp.mean(xf, -1, keepdims=True)
    var = jnp.mean(jnp.square(xf - mu), -1, keepdims=True)
    return ((xf - mu) * lax.rsqrt(var + LN_EPS) * g + b).astype(x.dtype)


def rms_norm(x, g):
    xf = x.astype(jnp.float32)
    return (xf * lax.rsqrt(jnp.mean(jnp.square(xf), -1, keepdims=True) + LN_EPS) * g).astype(x.dtype)


def swiglu(x, w_in, w_out):
    gate, up = jnp.split(x @ w_in, 2, axis=-1)
    return (jax.nn.silu(gate) * up) @ w_out


def rope_tables(positions):
    inv_freq = ROPE_THETA ** (-jnp.arange(0, A_QK_DIM, 2, dtype=jnp.float32) / A_QK_DIM)
    ang = positions.astype(jnp.float32)[..., None] * inv_freq
    ang = ang[:, :, None, None, :]
    return jnp.cos(ang), jnp.sin(ang)


def apply_rope(t, cos, sin):
    t1, t2 = jnp.split(t.astype(jnp.float32), 2, axis=-1)
    return jnp.concatenate([t1 * cos - t2 * sin, t2 * cos + t1 * sin], -1).astype(t.dtype)


def diff_attention(q, k, v, lam):
    B, S, H, _, dq = q.shape
    nb = S // Q_BLOCK
    scale = dq ** -0.5
    qb = q.reshape(B, nb, Q_BLOCK, H, 2, dq).transpose(1, 0, 2, 3, 4, 5)
    key_idx = jnp.arange(S)

    def block(args):
        q_blk, i = args
        s = jnp.einsum('bqhcd,bkhcd->bhcqk', q_blk, k).astype(jnp.float32) * scale
        q_idx = i * Q_BLOCK + jnp.arange(Q_BLOCK)
        s = jnp.where(key_idx[None, :] <= q_idx[:, None], s, -jnp.inf)
        p = jax.nn.softmax(s, axis=-1)
        w = p[:, :, 0] - lam * p[:, :, 1]
        return jnp.einsum('bhqk,bkhv->bqhv', w.astype(v.dtype), v)

    o = lax.map(block, (qb, jnp.arange(nb)))
    return o.transpose(1, 0, 2, 3, 4).reshape(B, S, H, v.shape[-1])


def multiscale_pool(u, pool_w, pool_scale):
    B, S, _ = u.shape
    uf = u.astype(jnp.float32)
    c = jnp.cumsum(uf, axis=1)
    t1 = jnp.arange(1, S + 1, dtype=jnp.float32)
    outs = []
    for g, w in enumerate(POOL_WINDOWS):
        sl = slice(g * POOL_GROUP, (g + 1) * POOL_GROUP)
        cg = c[..., sl]
        lagged = jnp.pad(cg[:, :S - w], ((0, 0), (w, 0), (0, 0)))
        mean = (cg - lagged) / jnp.minimum(t1, float(w))[None, :, None]
        outs.append(mean - uf[..., sl])
    d = jnp.stack(outs, axis=2).astype(u.dtype)
    y = jnp.einsum('bsgc,gcd->bsgd', d, pool_w)
    return y.reshape(B, S, B_WIDTH) * pool_scale


def diff_pool_mixer(x, cos, sin, w_in, w_o, lam_vecs, subln_g, pool_w, pool_scale, lambda_init):
    B, S, _ = x.shape
    h = x @ w_in
    q = apply_rope(h[..., :A_WIDTH].reshape(B, S, A_HEADS, 2, A_QK_DIM), cos, sin)
    k = apply_rope(h[..., A_WIDTH:2 * A_WIDTH].reshape(B, S, A_HEADS, 2, A_QK_DIM), cos, sin)
    v = h[..., 2 * A_WIDTH:3 * A_WIDTH].reshape(B, S, A_HEADS, A_V_DIM)
    u = h[..., 3 * A_WIDTH:]
    lv = lam_vecs.astype(jnp.float32)
    lam = jnp.exp(jnp.sum(lv[0] * lv[1])) - jnp.exp(jnp.sum(lv[2] * lv[3])) + lambda_init
    o = diff_attention(q, k, v, lam)
    o_a = (rms_norm(o, subln_g) * (1.0 - lambda_init)).reshape(B, S, A_WIDTH).astype(x.dtype)
    o_b = multiscale_pool(u, pool_w, pool_scale).astype(x.dtype)
    return jnp.concatenate([o_a, o_b], axis=-1) @ w_o


def hgrn2_chunk_scan(q, k, v, log_f):
    _, B, H, C, dk = q.shape
    dv = v.shape[-1]
    causal = jnp.tril(jnp.ones((C, C), dtype=bool))

    def step(state, inp):
        qc, kc, vc, lfc = inp
        b = jnp.cumsum(lfc, axis=-2)
        inter = jnp.einsum('bhtk,bhkv->bhtv', qc * jnp.exp(b), state)
        rel = b[:, :, :, None, :] - b[:, :, None, :, :]
        decay = jnp.exp(jnp.where(causal[:, :, None], rel, -jnp.inf))
        scores = jnp.einsum('bhtk,bhsk,bhtsk->bhts', qc, kc, decay)
        intra = jnp.einsum('bhts,bhsv->bhtv', scores, vc)
        b_last = b[:, :, -1:, :]
        state = jnp.exp(b_last[:, :, 0, :])[..., None] * state + jnp.einsum(
            'bhsk,bhsv->bhkv', kc * jnp.exp(b_last - b), vc)
        return state, inter + intra

    s0 = jnp.zeros((B, H, dk, dv), jnp.float32)
    _, o = lax.scan(step, s0, (q, k, v, log_f))
    return o


def hgrn2_mixer(x, w_in, w_o, g_norm, lower_bound):
    B, S, _ = x.shape
    q, fz, i, g = jnp.split(x @ w_in, 4, axis=-1)
    q = jax.nn.silu(q.astype(jnp.float32))
    f = lower_bound + (1.0 - lower_bound) * jax.nn.sigmoid(fz.astype(jnp.float32))
    log_f = jnp.log(f)
    k = 1.0 - f

    def heads(t):
        return t.astype(jnp.float32).reshape(B, S // CHUNK, CHUNK, C_HEADS, C_DIM).transpose(1, 0, 3, 2, 4)

    o = hgrn2_chunk_scan(heads(q), heads(k), heads(i), heads(log_f))
    o = o.transpose(1, 0, 3, 2, 4).reshape(B, S, C_HEADS, C_DIM)
    o = rms_norm(o, g_norm).reshape(B, S, D_MODEL).astype(x.dtype) * jax.nn.silu(g)
    return o @ w_o


def setup_inputs(seed: int = 0) -> dict:
    key = jax.random.key(seed)
    ks = jax.random.split(key, 20)

    def n(k, shape, s):
        return jax.random.normal(k, shape, jnp.float32) * s

    x = n(ks[0], (BATCH, SEQ, D_MODEL), 1.0)
    offsets = jax.random.randint(ks[1], (BATCH, 1), 0, 4096, dtype=jnp.int32)
    positions = offsets + jnp.arange(SEQ, dtype=jnp.int32)[None, :]
    return {
        "x": x,
        "positions": positions,
        "ln_gain": 1.0 + n(ks[2], (DEPTH, 3, D_MODEL), 0.05),
        "ln_bias": n(ks[3], (DEPTH, 3, D_MODEL), 0.02),
        "ffn1_w_in": n(ks[4], (DEPTH, D_MODEL, 2 * D_FF), D_MODEL ** -0.5),
        "ffn1_w_out": n(ks[5], (DEPTH, D_FF, D_MODEL), BETA * D_FF ** -0.5),
        "ffn2_w_in": n(ks[6], (DEPTH, D_MODEL, 2 * D_FF), D_MODEL ** -0.5),
        "ffn2_w_out": n(ks[7], (DEPTH, D_FF, D_MODEL), BETA * D_FF ** -0.5),
        "ab_w_in": n(ks[8], (N_AB, D_MODEL, AB_IN), D_MODEL ** -0.5),
        "ab_w_o": n(ks[9], (N_AB, A_WIDTH + B_WIDTH, D_MODEL), BETA * (A_WIDTH + B_WIDTH) ** -0.5),
        "diff_lambda": n(ks[10], (N_AB, 4, A_QK_DIM), 0.1),
        "diff_subln": 1.0 + n(ks[11], (N_AB, A_V_DIM), 0.05),
        "pool_w": n(ks[12], (N_AB, len(POOL_WINDOWS), POOL_GROUP, POOL_GROUP), POOL_GROUP ** -0.5),
        "pool_scale": 1.0 + n(ks[13], (N_AB, B_WIDTH), 0.1),
        "c_w_in": n(ks[14], (N_C, D_MODEL, C_IN), D_MODEL ** -0.5),
        "c_w_o": n(ks[15], (N_C, D_MODEL, D_MODEL), BETA * D_MODEL ** -0.5),
        "c_norm": 1.0 + n(ks[16], (N_C, C_DIM), 0.05),
        "hgrn_gamma": n(ks[17], (DEPTH, D_MODEL), 0.1),
    }


def reference(x, positions, ln_gain, ln_bias, ffn1_w_in, ffn1_w_out, ffn2_w_in, ffn2_w_out,
              ab_w_in, ab_w_o, diff_lambda, diff_subln, pool_w, pool_scale,
              c_w_in, c_w_o, c_norm, hgrn_gamma):
    cos, sin = rope_tables(positions)
    p = jax.nn.softmax(hgrn_gamma.astype(jnp.float32), axis=0)
    lower_bounds = jnp.cumsum(p, axis=0) - p[0]
    for l in range(DEPTH):
        j = l // 2
        x = layer_norm(ALPHA * x + 0.5 * swiglu(x, ffn1_w_in[l], ffn1_w_out[l]), ln_gain[l, 0], ln_bias[l, 0])
        if l % 2 == 0:
            lambda_init = 0.8 - 0.6 * math.exp(-0.3 * l)
            y = diff_pool_mixer(x, cos, sin, ab_w_in[j], ab_w_o[j], diff_lambda[j], diff_subln[j],
                                pool_w[j], pool_scale[j], lambda_init)
        else:
            y = hgrn2_mixer(x, c_w_in[j], c_w_o[j], c_norm[j], lower_bounds[l])
        x = layer_norm(ALPHA * x + y, ln_gain[l, 1], ln_bias[l, 1])
        x = layer_norm(ALPHA * x + 0.5 * swiglu(x, ffn2_w_in[l], ffn2_w_out[l]), ln_gain[l, 2], ln_bias[l, 2])
    return x
```

```python
import functools
import math

import numpy as np
import jax
import jax.numpy as jnp
from jax import lax
from jax.experimental import pallas as pl
from jax.experimental.pallas import tpu as pltpu

D_MODEL = 1024
DEPTH = 4
QK_DIM = 64
HEAD_W = 2 * QK_DIM
A_WIDTH = 512
A_HEADS = 4
ROPE_THETA = 10000.0
POOL_WINDOWS = (2, 4, 8, 16)
POOL_HALO = 16
B_WIDTH = 512
POOL_GROUP = 128
AB_IN = 3 * A_WIDTH + B_WIDTH
C_DIM = 128
C_HEADS = 8
C_IN = 4 * D_MODEL
CHUNK = 64
SUB = 16
D_FF = 2816
ALPHA = (2 * DEPTH) ** 0.25
LN_EPS = 1e-5

VMEM_LIMIT = 56 * 1024 * 1024

F32 = jnp.float32
BF16 = jnp.bfloat16
NT_DIMS = (((1,), (1,)), ((), ()))
TN_DIMS = (((0,), (0,)), ((), ()))


def _resident(shape):
    return pl.BlockSpec(shape, lambda *_: (0,) * len(shape), pipeline_mode=pl.Buffered(1))


def _layer_norm(z, g, b):
    mu = jnp.mean(z, axis=-1, keepdims=True)
    d = z - mu
    var = jnp.mean(d * d, axis=-1, keepdims=True)
    return d * lax.rsqrt(var + LN_EPS) * g + b


def _sigmoid(x):
    return 1.0 / (1.0 + jnp.exp(-x))


FFN_TM = 512
FFN_SUB = 256


def _ffn_kernel(x_ref, win_ref, wout_ref, g_ref, b_ref, o_ref):
    def body(i, carry):
        r = pl.multiple_of(i * FFN_SUB, FFN_SUB)
        x = x_ref[pl.ds(r, FFN_SUB), :]
        h = jnp.dot(x.astype(BF16), win_ref[...], preferred_element_type=F32)
        gate = h[:, :D_FF]
        up = h[:, D_FF:]
        act = (gate * _sigmoid(gate) * up).astype(BF16)
        y = jnp.dot(act, wout_ref[...], preferred_element_type=F32)
        o_ref[pl.ds(r, FFN_SUB), :] = _layer_norm(ALPHA * x + 0.5 * y, g_ref[...], b_ref[...])
        return carry

    lax.fori_loop(0, FFN_TM // FFN_SUB, body, 0)


def _ffn_ln(x, w_in, w_out, g, b):
    t = x.shape[0]
    return pl.pallas_call(
        _ffn_kernel,
        grid=(t // FFN_TM,),
        in_specs=[
            pl.BlockSpec((FFN_TM, D_MODEL), lambda i: (i, 0)),
            _resident((D_MODEL, 2 * D_FF)),
            _resident((D_FF, D_MODEL)),
            _resident((1, D_MODEL)),
            _resident((1, D_MODEL)),
        ],
        out_specs=pl.BlockSpec((FFN_TM, D_MODEL), lambda i: (i, 0)),
        out_shape=jax.ShapeDtypeStruct((t, D_MODEL), F32),
        compiler_params=pltpu.CompilerParams(
            dimension_semantics=("arbitrary",), vmem_limit_bytes=VMEM_LIMIT),
        name="ffn_ln",
    )(x, w_in, w_out, g, b)


AB_TM = 512


def _ab_in_kernel(x_ref, pos_ref, w_ref, freq_ref, pw_ref, ps_ref, o_ref, ubuf_ref):
    i = pl.program_id(1)
    h = jnp.dot(x_ref[...].astype(BF16), w_ref[...], preferred_element_type=F32)

    ang = pos_ref[...].astype(F32) * freq_ref[...]
    lane = lax.broadcasted_iota(jnp.int32, (AB_TM, HEAD_W), 1)
    first_half = (lane % QK_DIM) < (QK_DIM // 2)
    cos = jnp.cos(ang)
    sin = jnp.sin(ang)
    sin = jnp.where(first_half, -sin, sin)

    def rope(t):
        partner = jnp.where(first_half,
                            pltpu.roll(t, HEAD_W - QK_DIM // 2, axis=1),
                            pltpu.roll(t, QK_DIM // 2, axis=1))
        return t * cos + partner * sin

    for hd in range(A_HEADS):
        sl = slice(hd * HEAD_W, (hd + 1) * HEAD_W)
        q = rope(h[:, sl]) * (QK_DIM ** -0.5)
        o_ref[:, sl] = q.astype(BF16)
        ksl = slice(A_WIDTH + hd * HEAD_W, A_WIDTH + (hd + 1) * HEAD_W)
        o_ref[:, ksl] = rope(h[:, ksl]).astype(BF16)
    o_ref[:, 2 * A_WIDTH:3 * A_WIDTH] = h[:, 2 * A_WIDTH:3 * A_WIDTH].astype(BF16)

    u = h[:, 3 * A_WIDTH:]

    @pl.when(i == 0)
    def _():
        ubuf_ref[0:POOL_HALO, :] = jnp.zeros((POOL_HALO, B_WIDTH), F32)

    ubuf_ref[POOL_HALO:, :] = u
    seq_idx = i * AB_TM + lax.broadcasted_iota(jnp.int32, (AB_TM, 1), 0)
    count = (seq_idx + 1).astype(F32)
    for g, w in enumerate(POOL_WINDOWS):
        sl = slice(g * POOL_GROUP, (g + 1) * POOL_GROUP)
        acc = ubuf_ref[:, sl]
        step = 1
        while step < w:
            acc = acc + pltpu.roll(acc, step, axis=0)
            step *= 2
        mean = acc[POOL_HALO:, :] / jnp.minimum(count, float(w))
        d = (mean - u[:, sl]).astype(BF16)
        y = jnp.dot(d, pw_ref[g], preferred_element_type=F32) * ps_ref[:, sl]
        o_ref[:, 3 * A_WIDTH + g * POOL_GROUP:3 * A_WIDTH + (g + 1) * POOL_GROUP] = y.astype(BF16)
    ubuf_ref[0:POOL_HALO, :] = u[AB_TM - POOL_HALO:, :]


def _ab_in(x, pos, w_in, freq, pool_w, pool_scale, batch):
    t = x.shape[0]
    nt = t // batch // AB_TM
    return pl.pallas_call(
        _ab_in_kernel,
        grid=(batch, nt),
        in_specs=[
            pl.BlockSpec((AB_TM, D_MODEL), lambda b, i: (b * nt + i, 0)),
            pl.BlockSpec((AB_TM, 1), lambda b, i: (b * nt + i, 0)),
            _resident((D_MODEL, AB_IN)),
            _resident((1, HEAD_W)),
            _resident((len(POOL_WINDOWS), POOL_GROUP, POOL_GROUP)),
            _resident((1, B_WIDTH)),
        ],
        out_specs=pl.BlockSpec((AB_TM, AB_IN), lambda b, i: (b * nt + i, 0)),
        out_shape=jax.ShapeDtypeStruct((t, AB_IN), BF16),
        scratch_shapes=[pltpu.VMEM((POOL_HALO + AB_TM, B_WIDTH), F32)],
        compiler_params=pltpu.CompilerParams(
            dimension_semantics=("arbitrary", "arbitrary"), vmem_limit_bytes=VMEM_LIMIT),
        name="ab_in",
    )(x, pos, w_in, freq, pool_w, pool_scale)


ATT_TQ = 256
ATT_TK = 256


def _attn_kernel(q_ref, k_ref, v_ref, ob_ref, x_ref, wo_ref, lam_ref, sub_ref, g_ref, b_ref,
                 o_ref, m_ref, l_ref, acc_ref, oa_ref, *, lambda_init):
    qi = pl.program_id(1)
    lv = lam_ref[...]
    lam = (jnp.exp(jnp.sum(lv[0:1] * lv[1:2], axis=-1, keepdims=True))
           - jnp.exp(jnp.sum(lv[2:3] * lv[3:4], axis=-1, keepdims=True)) + lambda_init)

    lane = lax.broadcasted_iota(jnp.int32, (ATT_TQ, HEAD_W), 1)
    row = lax.broadcasted_iota(jnp.int32, (ATT_TQ, ATT_TK), 0)
    col = lax.broadcasted_iota(jnp.int32, (ATT_TQ, ATT_TK), 1)

    for hd in range(A_HEADS):
        sl = slice(hd * HEAD_W, (hd + 1) * HEAD_W)
        qh = q_ref[:, sl]
        qmaps = (jnp.where(lane < QK_DIM, qh, jnp.zeros_like(qh)),
                 jnp.where(lane >= QK_DIM, qh, jnp.zeros_like(qh)))
        m_ref[...] = jnp.full(m_ref.shape, -jnp.inf, F32)
        l_ref[...] = jnp.zeros(l_ref.shape, F32)
        acc_ref[...] = jnp.zeros(acc_ref.shape, F32)

        def block(j, masked):
            r = pl.multiple_of(j * ATT_TK, ATT_TK)
            kb = k_ref[pl.ds(r, ATT_TK), sl]
            vb = v_ref[pl.ds(r, ATT_TK), sl]
            for c in range(2):
                s = lax.dot_general(qmaps[c], kb, NT_DIMS, preferred_element_type=F32)
                if masked:
                    s = jnp.where(col <= row, s, -jnp.inf)
                m_old = m_ref[c]
                m_new = jnp.maximum(m_old, jnp.max(s, axis=-1, keepdims=True))
                scale = jnp.exp(m_old - m_new)
                p = jnp.exp(s - m_new)
                l_ref[c] = scale * l_ref[c] + jnp.sum(p, axis=-1, keepdims=True)
                acc_ref[c] = scale * acc_ref[c] + jnp.dot(p.astype(BF16), vb, preferred_element_type=F32)
                m_ref[c] = m_new

        def loop_body(j, carry):
            block(j, False)
            return carry

        lax.fori_loop(0, qi, loop_body, 0)
        block(qi, True)

        o = acc_ref[0] / l_ref[0] - lam * (acc_ref[1] / l_ref[1])
        o = o * lax.rsqrt(jnp.mean(o * o, axis=-1, keepdims=True) + LN_EPS) * sub_ref[...]
        oa_ref[:, sl] = (o * (1.0 - lambda_init)).astype(BF16)

    y = jnp.dot(oa_ref[...], wo_ref[0:A_WIDTH, :], preferred_element_type=F32)
    y = y + jnp.dot(ob_ref[...], wo_ref[A_WIDTH:, :], preferred_element_type=F32)
    o_ref[...] = _layer_norm(ALPHA * x_ref[...] + y, g_ref[...], b_ref[...])


def _attn_out(qkvo, x, w_o, lam_vecs, subln, g, b, batch, lambda_init):
    t = x.shape[0]
    seq = t // batch
    nq = seq // ATT_TQ
    return pl.pallas_call(
        functools.partial(_attn_kernel, lambda_init=lambda_init),
        grid=(batch, nq),
        in_specs=[
            pl.BlockSpec((ATT_TQ, A_WIDTH), lambda bb, i: (bb * nq + i, 0)),
            pl.BlockSpec((seq, A_WIDTH), lambda bb, i: (bb, 1), pipeline_mode=pl.Buffered(1)),
            pl.BlockSpec((seq, A_WIDTH), lambda bb, i: (bb, 2), pipeline_mode=pl.Buffered(1)),
            pl.BlockSpec((ATT_TQ, B_WIDTH), lambda bb, i: (bb * nq + i, 3)),
            pl.BlockSpec((ATT_TQ, D_MODEL), lambda bb, i: (bb * nq + i, 0)),
            _resident((D_MODEL, D_MODEL)),
            _resident((4, QK_DIM)),
            _resident((1, HEAD_W)),
            _resident((1, D_MODEL)),
            _resident((1, D_MODEL)),
        ],
        out_specs=pl.BlockSpec((ATT_TQ, D_MODEL), lambda bb, i: (bb * nq + i, 0)),
        out_shape=jax.ShapeDtypeStruct((t, D_MODEL), F32),
        scratch_shapes=[
            pltpu.VMEM((2, ATT_TQ, 1), F32),
            pltpu.VMEM((2, ATT_TQ, 1), F32),
            pltpu.VMEM((2, ATT_TQ, HEAD_W), F32),
            pltpu.VMEM((ATT_TQ, A_WIDTH), BF16),
        ],
        compiler_params=pltpu.CompilerParams(
            dimension_semantics=("arbitrary", "arbitrary"), vmem_limit_bytes=VMEM_LIMIT),
        name="diff_attn_out",
    )(qkvo, qkvo, qkvo, qkvo, x, w_o, lam_vecs, subln, g, b)


C_TM = 256
N_SUB = CHUNK // SUB


def _hgrn_kernel(x_ref, win_ref, wo_ref, gn_ref, gamma_ref, g_ref, b_ref, o_ref,
                 h_ref, state_ref, obuf_ref, q_ref, k_ref, bc_ref, v_ref, od_ref,
                 qt_ref, kbar_ref, qhat_ref, khat_ref, vb_ref, *, layer):
    @pl.when(pl.program_id(1) == 0)
    def _():
        state_ref[...] = jnp.zeros(state_ref.shape, F32)

    gam = gamma_ref[...]
    e = jnp.exp(gam - jnp.max(gam, axis=0, keepdims=True))
    p = e / jnp.sum(e, axis=0, keepdims=True)
    lb = jnp.sum(p[1:layer + 1], axis=0, keepdims=True)

    x = x_ref[...]
    h_ref[...] = jnp.dot(x.astype(BF16), win_ref[...], preferred_element_type=F32)

    tri = (lax.broadcasted_iota(jnp.int32, (CHUNK, CHUNK), 1)
           <= lax.broadcasted_iota(jnp.int32, (CHUNK, CHUNK), 0)).astype(F32)
    sub_row = lax.broadcasted_iota(jnp.int32, (SUB, 1), 0)

    def chunk_body(c, carry):
        r0 = pl.multiple_of(c * CHUNK, CHUNK)
        hq = h_ref[pl.ds(r0, CHUNK), 0:D_MODEL]
        q = hq * _sigmoid(hq)
        f = lb + (1.0 - lb) * _sigmoid(h_ref[pl.ds(r0, CHUNK), D_MODEL:2 * D_MODEL])
        logf = jnp.log(f)
        kk = 1.0 - f
        iv = h_ref[pl.ds(r0, CHUNK), 2 * D_MODEL:3 * D_MODEL]
        bc = jnp.dot(tri, logf, preferred_element_type=F32, precision=lax.Precision.HIGHEST)
        b_last = bc[CHUNK - 1:CHUNK, :]
        q_ref[...] = q
        k_ref[...] = kk
        bc_ref[...] = bc
        v_ref[...] = iv
        qt_ref[...] = (q * jnp.exp(bc)).astype(BF16)
        kbar_ref[...] = (kk * jnp.exp(b_last - bc)).astype(BF16)
        vb_ref[...] = iv.astype(BF16)

        for i in range(1, N_SUB):
            rows = slice(i * SUB, (i + 1) * SUB)
            ref_i = bc[i * SUB - 1:i * SUB, :]
            qhat_ref[rows, :] = (q[rows] * jnp.exp(bc[rows] - ref_i)).astype(BF16)
            prev = slice(0, i * SUB)
            khat_ref[i - 1, prev, :] = (kk[prev] * jnp.exp(ref_i - bc[prev])).astype(BF16)
            khat_ref[i - 1, i * SUB:, :] = jnp.zeros((CHUNK - i * SUB, D_MODEL), BF16)

        for i in range(N_SUB):
            a = i * SUB
            q_s = q_ref[a:a + SUB, :]
            b_s = bc_ref[a:a + SUB, :]
            acc = [jnp.zeros((SUB, C_DIM), F32) for _ in range(C_HEADS)]
            for s in range(SUB):
                k_row = k_ref[a + s:a + s + 1, :]
                b_row = bc_ref[a + s:a + s + 1, :]
                v_row = v_ref[a + s:a + s + 1, :]
                e_ts = q_s * k_row * jnp.exp(jnp.minimum(b_s - b_row, 0.0))
                e_ts = jnp.where(sub_row >= s, e_ts, 0.0)
                for hd in range(C_HEADS):
                    hl = slice(hd * C_DIM, (hd + 1) * C_DIM)
                    w_ts = jnp.sum(e_ts[:, hl], axis=-1, keepdims=True)
                    acc[hd] = acc[hd] + w_ts * v_row[:, hl]
            for hd in range(C_HEADS):
                od_ref[a:a + SUB, hd * C_DIM:(hd + 1) * C_DIM] = acc[hd]

        for hd in range(C_HEADS):
            hl = slice(hd * C_DIM, (hd + 1) * C_DIM)
            st = state_ref[hd * C_DIM:(hd + 1) * C_DIM, :]
            inter = lax.dot_general(qt_ref[:, hl], st.astype(BF16), NT_DIMS, preferred_element_type=F32)
            blocks = [jnp.zeros((SUB, CHUNK), F32)]
            for i in range(1, N_SUB):
                blocks.append(lax.dot_general(qhat_ref[i * SUB:(i + 1) * SUB, hl], khat_ref[i - 1, :, hl],
                                              NT_DIMS, preferred_element_type=F32))
            s_off = jnp.concatenate(blocks, axis=0).astype(BF16)
            intra = jnp.dot(s_off, vb_ref[:, hl], preferred_element_type=F32)
            obuf_ref[pl.ds(r0, CHUNK), hl] = inter + intra + od_ref[:, hl]
            upd = lax.dot_general(vb_ref[:, hl], kbar_ref[:, hl], TN_DIMS, preferred_element_type=F32)
            state_ref[hd * C_DIM:(hd + 1) * C_DIM, :] = st * jnp.exp(b_last[:, hl]) + upd
        return carry

    lax.fori_loop(0, C_TM // CHUNK, chunk_body, 0)

    gz = h_ref[:, 3 * D_MODEL:]
    gate = gz * _sigmoid(gz)
    for hd in range(C_HEADS):
        hl = slice(hd * C_DIM, (hd + 1) * C_DIM)
        oh = obuf_ref[:, hl]
        oh = oh * lax.rsqrt(jnp.mean(oh * oh, axis=-1, keepdims=True) + LN_EPS) * gn_ref[...]
        obuf_ref[:, hl] = oh * gate[:, hl]
    y = jnp.dot(obuf_ref[...].astype(BF16), wo_ref[...], preferred_element_type=F32)
    o_ref[...] = _layer_norm(ALPHA * x + y, g_ref[...], b_ref[...])


def _hgrn_layer(x, w_in, w_o, g_norm, gamma, g, b, batch, layer):
    t = x.shape[0]
    nt = t // batch // C_TM
    return pl.pallas_call(
        functools.partial(_hgrn_kernel, layer=layer),
        grid=(batch, nt),
        in_specs=[
            pl.BlockSpec((C_TM, D_MODEL), lambda bb, i: (bb * nt + i, 0)),
            _resident((D_MODEL, C_IN)),
            _resident((D_MODEL, D_MODEL)),
            _resident((1, C_DIM)),
            _resident((DEPTH, D_MODEL)),
            _resident((1, D_MODEL)),
            _resident((1, D_MODEL)),
        ],
        out_specs=pl.BlockSpec((C_TM, D_MODEL), lambda bb, i: (bb * nt + i, 0)),
        out_shape=jax.ShapeDtypeStruct((t, D_MODEL), F32),
        scratch_shapes=[
            pltpu.VMEM((C_TM, C_IN), F32),
            pltpu.VMEM((C_HEADS * C_DIM, C_DIM), F32),
            pltpu.VMEM((C_TM, D_MODEL), F32),
            pltpu.VMEM((CHUNK, D_MODEL), F32),
            pltpu.VMEM((CHUNK, D_MODEL), F32),
            pltpu.VMEM((CHUNK, D_MODEL), F32),
            pltpu.VMEM((CHUNK, D_MODEL), F32),
            pltpu.VMEM((CHUNK, D_MODEL), F32),
            pltpu.VMEM((CHUNK, D_MODEL), BF16),
            pltpu.VMEM((CHUNK, D_MODEL), BF16),
            pltpu.VMEM((CHUNK, D_MODEL), BF16),
            pltpu.VMEM((N_SUB - 1, CHUNK, D_MODEL), BF16),
            pltpu.VMEM((CHUNK, D_MODEL), BF16),
        ],
        compiler_params=pltpu.CompilerParams(
            dimension_semantics=("arbitrary", "arbitrary"), vmem_limit_bytes=VMEM_LIMIT),
        name="hgrn2_layer",
    )(x, w_in, w_o, g_norm, gamma, g, b)


def kernel(x, positions, ln_gain, ln_bias, ffn1_w_in, ffn1_w_out, ffn2_w_in, ffn2_w_out,
           ab_w_in, ab_w_o, diff_lambda, diff_subln, pool_w, pool_scale,
           c_w_in, c_w_o, c_norm, hgrn_gamma):
    batch, seq, d = x.shape
    t = batch * seq
    x = x.reshape(t, d)
    pos = positions.reshape(t, 1)
    inv_freq = ROPE_THETA ** (-jnp.arange(0, QK_DIM, 2, dtype=F32) / QK_DIM)
    freq = jnp.tile(inv_freq, HEAD_W // (QK_DIM // 2)).reshape(1, HEAD_W)

    def row(a):
        return a.reshape(1, -1)

    for l in range(DEPTH):
        j = l // 2
        x = _ffn_ln(x, ffn1_w_in[l].astype(BF16), ffn1_w_out[l].astype(BF16),
                    row(ln_gain[l, 0]), row(ln_bias[l, 0]))
        if l % 2 == 0:
            lambda_init = 0.8 - 0.6 * math.exp(-0.3 * l)
            qkvo = _ab_in(x, pos, ab_w_in[j].astype(BF16), freq, pool_w[j].astype(BF16),
                          row(pool_scale[j]), batch)
            x = _attn_out(qkvo, x, ab_w_o[j].astype(BF16), diff_lambda[j], row(diff_subln[j]),
                          row(ln_gain[l, 1]), row(ln_bias[l, 1]), batch, lambda_init)
        else:
            x = _hgrn_layer(x, c_w_in[j].astype(BF16), c_w_o[j].astype(BF16), row(c_norm[j]),
                            hgrn_gamma, row(ln_gain[l, 1]), row(ln_bias[l, 1]), batch, l)
        x = _ffn_ln(x, ffn2_w_in[l].astype(BF16), ffn2_w_out[l].astype(BF16),
                    row(ln_gain[l, 2]), row(ln_bias[l, 2]))
    return x.reshape(batch, seq, d)
```

```python
import functools
import math

import numpy as np
import jax
import jax.numpy as jnp
from jax import lax
from jax.experimental import pallas as pl
from jax.experimental.pallas import tpu as pltpu

D_MODEL = 1024
DEPTH = 4
QK_DIM = 64
HEAD_W = 2 * QK_DIM
A_WIDTH = 512
A_HEADS = 4
ROPE_THETA = 10000.0
POOL_WINDOWS = (2, 4, 8, 16)
POOL_HALO = 16
B_WIDTH = 512
POOL_GROUP = 128
AB_IN = 3 * A_WIDTH + B_WIDTH
QKU_W = 2 * A_WIDTH + B_WIDTH
C_DIM = 128
C_HEADS = 8
C_IN = 4 * D_MODEL
CHUNK = 64
SUB = 16
D_FF = 2816
ALPHA = (2 * DEPTH) ** 0.25
LN_EPS = 1e-5

VMEM_LIMIT = 56 * 1024 * 1024

F32 = jnp.float32
BF16 = jnp.bfloat16
NT_DIMS = (((1,), (1,)), ((), ()))
TN_DIMS = (((0,), (0,)), ((), ()))


def _resident(shape):
    return pl.BlockSpec(shape, lambda *_: (0,) * len(shape), pipeline_mode=pl.Buffered(1))


def _layer_norm(z, g, b):
    mu = jnp.mean(z, axis=-1, keepdims=True)
    d = z - mu
    var = jnp.mean(d * d, axis=-1, keepdims=True)
    return d * lax.rsqrt(var + LN_EPS) * g + b


def _sigmoid(x):
    return 1.0 / (1.0 + jnp.exp(-x))


FFN_TM = 512
FFN_SUB = 256


def _ffn_kernel(x_ref, win_ref, wout_ref, g_ref, b_ref, o_ref):
    def body(i, carry):
        r = pl.multiple_of(i * FFN_SUB, FFN_SUB)
        x = x_ref[pl.ds(r, FFN_SUB), :]
        h = jnp.dot(x.astype(BF16), win_ref[...], preferred_element_type=F32)
        gate = h[:, :D_FF]
        up = h[:, D_FF:]
        act = (gate * _sigmoid(gate) * up).astype(BF16)
        y = jnp.dot(act, wout_ref[...], preferred_element_type=F32)
        o_ref[pl.ds(r, FFN_SUB), :] = _layer_norm(ALPHA * x + 0.5 * y, g_ref[...], b_ref[...])
        return carry

    lax.fori_loop(0, FFN_TM // FFN_SUB, body, 0)


def _ffn_ln(x, w_in, w_out, g, b):
    t = x.shape[0]
    return pl.pallas_call(
        _ffn_kernel,
        grid=(t // FFN_TM,),
        in_specs=[
            pl.BlockSpec((FFN_TM, D_MODEL), lambda i: (i, 0)),
            _resident((D_MODEL, 2 * D_FF)),
            _resident((D_FF, D_MODEL)),
            _resident((1, D_MODEL)),
            _resident((1, D_MODEL)),
        ],
        out_specs=pl.BlockSpec((FFN_TM, D_MODEL), lambda i: (i, 0)),
        out_shape=jax.ShapeDtypeStruct((t, D_MODEL), F32),
        compiler_params=pltpu.CompilerParams(
            dimension_semantics=("arbitrary",), vmem_limit_bytes=VMEM_LIMIT),
        name="ffn_ln",
    )(x, w_in, w_out, g, b)


AB_TM = 512
Q_SCALE = QK_DIM ** -0.5 * math.log2(math.e)


def _ab_in_kernel(x_ref, pos_ref, w_ref, wvt_ref, freq_ref, pw_ref, ps_ref, o_ref, vt_ref, ubuf_ref):
    i = pl.program_id(1)
    xb = x_ref[...].astype(BF16)
    h = jnp.dot(xb, w_ref[...], preferred_element_type=F32)
    vt_ref[...] = lax.dot_general(wvt_ref[...], xb, NT_DIMS, preferred_element_type=F32).astype(BF16)

    ang = pos_ref[...].astype(F32) * freq_ref[...]
    lane = lax.broadcasted_iota(jnp.int32, (AB_TM, HEAD_W), 1)
    first_half = (lane % QK_DIM) < (QK_DIM // 2)
    cos = jnp.cos(ang)
    sin = jnp.sin(ang)
    sin = jnp.where(first_half, -sin, sin)

    def rope(t):
        partner = jnp.where(first_half,
                            pltpu.roll(t, HEAD_W - QK_DIM // 2, axis=1),
                            pltpu.roll(t, QK_DIM // 2, axis=1))
        return t * cos + partner * sin

    for hd in range(A_HEADS):
        sl = slice(hd * HEAD_W, (hd + 1) * HEAD_W)
        q = rope(h[:, sl]) * Q_SCALE
        o_ref[:, sl] = q.astype(BF16)
        ksl = slice(A_WIDTH + hd * HEAD_W, A_WIDTH + (hd + 1) * HEAD_W)
        o_ref[:, ksl] = rope(h[:, ksl]).astype(BF16)

    u = h[:, 2 * A_WIDTH:]

    @pl.when(i == 0)
    def _():
        ubuf_ref[0:POOL_HALO, :] = jnp.zeros((POOL_HALO, B_WIDTH), F32)

    ubuf_ref[POOL_HALO:, :] = u
    seq_idx = i * AB_TM + lax.broadcasted_iota(jnp.int32, (AB_TM, 1), 0)
    count = (seq_idx + 1).astype(F32)
    for g, w in enumerate(POOL_WINDOWS):
        sl = slice(g * POOL_GROUP, (g + 1) * POOL_GROUP)
        acc = ubuf_ref[:, sl]
        step = 1
        while step < w:
            acc = acc + pltpu.roll(acc, step, axis=0)
            step *= 2
        mean = acc[POOL_HALO:, :] / jnp.minimum(count, float(w))
        d = (mean - u[:, sl]).astype(BF16)
        y = jnp.dot(d, pw_ref[g], preferred_element_type=F32) * ps_ref[:, sl]
        o_ref[:, 2 * A_WIDTH + g * POOL_GROUP:2 * A_WIDTH + (g + 1) * POOL_GROUP] = y.astype(BF16)
    ubuf_ref[0:POOL_HALO, :] = u[AB_TM - POOL_HALO:, :]


def _ab_in(x, pos, w_qku, w_vt, freq, pool_w, pool_scale, batch):
    t = x.shape[0]
    seq = t // batch
    nt = seq // AB_TM
    return pl.pallas_call(
        _ab_in_kernel,
        grid=(batch, nt),
        in_specs=[
            pl.BlockSpec((AB_TM, D_MODEL), lambda b, i: (b * nt + i, 0)),
            pl.BlockSpec((AB_TM, 1), lambda b, i: (b * nt + i, 0)),
            _resident((D_MODEL, QKU_W)),
            _resident((A_WIDTH, D_MODEL)),
            _resident((1, HEAD_W)),
            _resident((len(POOL_WINDOWS), POOL_GROUP, POOL_GROUP)),
            _resident((1, B_WIDTH)),
        ],
        out_specs=[
            pl.BlockSpec((AB_TM, QKU_W), lambda b, i: (b * nt + i, 0)),
            pl.BlockSpec((None, A_WIDTH, AB_TM), lambda b, i: (b, 0, i)),
        ],
        out_shape=[
            jax.ShapeDtypeStruct((t, QKU_W), BF16),
            jax.ShapeDtypeStruct((batch, A_WIDTH, seq), BF16),
        ],
        scratch_shapes=[pltpu.VMEM((POOL_HALO + AB_TM, B_WIDTH), F32)],
        compiler_params=pltpu.CompilerParams(
            dimension_semantics=("arbitrary", "arbitrary"), vmem_limit_bytes=VMEM_LIMIT),
        name="ab_in",
    )(x, pos, w_qku, w_vt, freq, pool_w, pool_scale)


ATT_TQ = 256
ATT_TK = 256
N_MAPS = 2 * A_HEADS


def _attn_kernel(q_ref, k_ref, vt_ref, ob_ref, x_ref, wo_ref, lam_ref, sub_ref, g_ref, b_ref,
                 o_ref, acc_ref, oat_ref, qz_ref, s_ref, p_ref, m_ref, l_ref, scale_ref, *, lambda_init):
    qi = pl.program_id(1)
    lv = lam_ref[...]
    lam = (jnp.exp(jnp.sum(lv[0:1] * lv[1:2], axis=-1, keepdims=True))
           - jnp.exp(jnp.sum(lv[2:3] * lv[3:4], axis=-1, keepdims=True)) + lambda_init)

    lane = lax.broadcasted_iota(jnp.int32, (ATT_TQ, HEAD_W), 1)
    key_idx = lax.broadcasted_iota(jnp.int32, (ATT_TK, ATT_TQ), 0)
    qry_idx = lax.broadcasted_iota(jnp.int32, (ATT_TK, ATT_TQ), 1)

    heads = [slice(hd * HEAD_W, (hd + 1) * HEAD_W) for hd in range(A_HEADS)]
    for hd, sl in enumerate(heads):
        qh = q_ref[:, sl]
        qz_ref[2 * hd] = jnp.where(lane < QK_DIM, qh, jnp.zeros_like(qh))
        qz_ref[2 * hd + 1] = jnp.where(lane >= QK_DIM, qh, jnp.zeros_like(qh))
    acc_ref[...] = jnp.zeros(acc_ref.shape, F32)
    p_ref[...] = jnp.zeros(p_ref.shape, BF16)
    m_ref[...] = jnp.full(m_ref.shape, -jnp.inf, F32)
    l_ref[...] = jnp.zeros(l_ref.shape, F32)
    scale_ref[...] = jnp.ones(scale_ref.shape, F32)

    def scores(j, i):
        r = pl.multiple_of(j * ATT_TK, ATT_TK)
        kb = k_ref[pl.ds(r, ATT_TK), heads[i // 2]]
        s_ref[i] = lax.dot_general(kb, qz_ref[i], NT_DIMS, preferred_element_type=F32)

    def softmax(i, masked):
        sc = s_ref[i]
        if masked:
            sc = jnp.where(key_idx <= qry_idx, sc, -jnp.inf)
        m_old = m_ref[i:i + 1, :]
        m_new = jnp.maximum(m_old, jnp.max(sc, axis=0, keepdims=True))
        scale = jnp.exp2(m_old - m_new)
        p = jnp.exp2(sc - m_new)
        l_ref[i:i + 1, :] = scale * l_ref[i:i + 1, :] + jnp.sum(p, axis=0, keepdims=True)
        m_ref[i:i + 1, :] = m_new
        scale_ref[i:i + 1, :] = scale
        p_ref[i] = p.astype(BF16)

    def accumulate(j, i):
        r = pl.multiple_of(j * ATT_TK, ATT_TK)
        vtb = vt_ref[heads[i // 2], pl.ds(r, ATT_TK)]
        acc_ref[i] = (scale_ref[i:i + 1, :] * acc_ref[i]
                      + jnp.dot(vtb, p_ref[i], preferred_element_type=F32))

    def step(j, carry):
        for i in range(N_MAPS):
            accumulate(jnp.maximum(j - 1, 0), i)
            softmax(i, False)
            scores(j + 1, i)
        return carry

    for i in range(N_MAPS):
        scores(0, i)
    lax.fori_loop(0, qi, step, 0)
    for i in range(N_MAPS):
        accumulate(jnp.maximum(qi - 1, 0), i)
        softmax(i, True)
        accumulate(qi, i)

    for hd, sl in enumerate(heads):
        o = (acc_ref[2 * hd] / l_ref[2 * hd:2 * hd + 1, :]
             - lam * (acc_ref[2 * hd + 1] / l_ref[2 * hd + 1:2 * hd + 2, :]))
        o = o * lax.rsqrt(jnp.mean(o * o, axis=0, keepdims=True) + LN_EPS) * sub_ref[...]
        oat_ref[sl, :] = (o * (1.0 - lambda_init)).astype(BF16)

    y = lax.dot_general(oat_ref[...], wo_ref[0:A_WIDTH, :], TN_DIMS, preferred_element_type=F32)
    y = y + jnp.dot(ob_ref[...], wo_ref[A_WIDTH:, :], preferred_element_type=F32)
    o_ref[...] = _layer_norm(ALPHA * x_ref[...] + y, g_ref[...], b_ref[...])


def _attn_out(qku, vt, x, w_o, lam_vecs, subln, g, b, batch, lambda_init):
    t = x.shape[0]
    seq = t // batch
    nq = seq // ATT_TQ
    return pl.pallas_call(
        functools.partial(_attn_kernel, lambda_init=lambda_init),
        grid=(batch, nq),
        in_specs=[
            pl.BlockSpec((ATT_TQ, A_WIDTH), lambda bb, i: (bb * nq + i, 0)),
            pl.BlockSpec((seq, A_WIDTH), lambda bb, i: (bb, 1), pipeline_mode=pl.Buffered(1)),
            pl.BlockSpec((None, A_WIDTH, seq), lambda bb, i: (bb, 0, 0), pipeline_mode=pl.Buffered(1)),
            pl.BlockSpec((ATT_TQ, B_WIDTH), lambda bb, i: (bb * nq + i, 2)),
            pl.BlockSpec((ATT_TQ, D_MODEL), lambda bb, i: (bb * nq + i, 0)),
            _resident((D_MODEL, D_MODEL)),
            _resident((4, QK_DIM)),
            _resident((HEAD_W, 1)),
            _resident((1, D_MODEL)),
            _resident((1, D_MODEL)),
        ],
        out_specs=pl.BlockSpec((ATT_TQ, D_MODEL), lambda bb, i: (bb * nq + i, 0)),
        out_shape=jax.ShapeDtypeStruct((t, D_MODEL), F32),
        scratch_shapes=[
            pltpu.VMEM((N_MAPS, HEAD_W, ATT_TQ), F32),
            pltpu.VMEM((A_WIDTH, ATT_TQ), BF16),
            pltpu.VMEM((N_MAPS, ATT_TQ, HEAD_W), BF16),
            pltpu.VMEM((N_MAPS, ATT_TK, ATT_TQ), F32),
            pltpu.VMEM((N_MAPS, ATT_TK, ATT_TQ), BF16),
            pltpu.VMEM((N_MAPS, ATT_TQ), F32),
            pltpu.VMEM((N_MAPS, ATT_TQ), F32),
            pltpu.VMEM((N_MAPS, ATT_TQ), F32),
        ],
        compiler_params=pltpu.CompilerParams(
            dimension_semantics=("arbitrary", "arbitrary"), vmem_limit_bytes=VMEM_LIMIT),
        name="diff_attn_out",
    )(qku, qku, vt, qku, x, w_o, lam_vecs, subln, g, b)


C_TM = 256
N_SUB = CHUNK // SUB


def _hgrn_kernel(x_ref, win_ref, wo_ref, gn_ref, gamma_ref, g_ref, b_ref, o_ref,
                 h_ref, state_ref, obuf_ref, q_ref, k_ref, bc_ref, v_ref, od_ref,
                 qt_ref, kbar_ref, qhat_ref, khat_ref, vb_ref, *, layer):
    @pl.when(pl.program_id(1) == 0)
    def _():
        state_ref[...] = jnp.zeros(state_ref.shape, F32)

    gam = gamma_ref[...]
    e = jnp.exp(gam - jnp.max(gam, axis=0, keepdims=True))
    p = e / jnp.sum(e, axis=0, keepdims=True)
    lb = jnp.sum(p[1:layer + 1], axis=0, keepdims=True)

    x = x_ref[...]
    h_ref[...] = jnp.dot(x.astype(BF16), win_ref[...], preferred_element_type=F32)

    tri = (lax.broadcasted_iota(jnp.int32, (CHUNK, CHUNK), 1)
           <= lax.broadcasted_iota(jnp.int32, (CHUNK, CHUNK), 0)).astype(F32)
    sub_row = lax.broadcasted_iota(jnp.int32, (SUB, 1), 0)

    def chunk_body(c, carry):
        r0 = pl.multiple_of(c * CHUNK, CHUNK)
        hq = h_ref[pl.ds(r0, CHUNK), 0:D_MODEL]
        q = hq * _sigmoid(hq)
        f = lb + (1.0 - lb) * _sigmoid(h_ref[pl.ds(r0, CHUNK), D_MODEL:2 * D_MODEL])
        logf = jnp.log(f)
        kk = 1.0 - f
        iv = h_ref[pl.ds(r0, CHUNK), 2 * D_MODEL:3 * D_MODEL]
        bc = jnp.dot(tri, logf, preferred_element_type=F32, precision=lax.Precision.HIGHEST)
        b_last = bc[CHUNK - 1:CHUNK, :]
        q_ref[...] = q
        k_ref[...] = kk
        bc_ref[...] = bc
        v_ref[...] = iv
        qt_ref[...] = (q * jnp.exp(bc)).astype(BF16)
        kbar_ref[...] = (kk * jnp.exp(b_last - bc)).astype(BF16)
        vb_ref[...] = iv.astype(BF16)

        for i in range(1, N_SUB):
            rows = slice(i * SUB, (i + 1) * SUB)
            ref_i = bc[i * SUB - 1:i * SUB, :]
            qhat_ref[rows, :] = (q[rows] * jnp.exp(bc[rows] - ref_i)).astype(BF16)
            prev = slice(0, i * SUB)
            khat_ref[i - 1, prev, :] = (kk[prev] * jnp.exp(ref_i - bc[prev])).astype(BF16)
            khat_ref[i - 1, i * SUB:, :] = jnp.zeros((CHUNK - i * SUB, D_MODEL), BF16)

        for i in range(N_SUB):
            a = i * SUB
            q_s = q_ref[a:a + SUB, :]
            b_s = bc_ref[a:a + SUB, :]
            acc = [jnp.zeros((SUB, C_DIM), F32) for _ in range(C_HEADS)]
            for s in range(SUB):
                k_row = k_ref[a + s:a + s + 1, :]
                b_row = bc_ref[a + s:a + s + 1, :]
                v_row = v_ref[a + s:a + s + 1, :]
                e_ts = q_s * k_row * jnp.exp(jnp.minimum(b_s - b_row, 0.0))
                e_ts = jnp.where(sub_row >= s, e_ts, 0.0)
                for hd in range(C_HEADS):
                    hl = slice(hd * C_DIM, (hd + 1) * C_DIM)
                    w_ts = jnp.sum(e_ts[:, hl], axis=-1, keepdims=True)
                    acc[hd] = acc[hd] + w_ts * v_row[:, hl]
            for hd in range(C_HEADS):
                od_ref[a:a + SUB, hd * C_DIM:(hd + 1) * C_DIM] = acc[hd]

        for hd in range(C_HEADS):
            hl = slice(hd * C_DIM, (hd + 1) * C_DIM)
            st = state_ref[hd * C_DIM:(hd + 1) * C_DIM, :]
            inter = lax.dot_general(qt_ref[:, hl], st.astype(BF16), NT_DIMS, preferred_element_type=F32)
            blocks = [jnp.zeros((SUB, CHUNK), F32)]
            for i in range(1, N_SUB):
                blocks.append(lax.dot_general(qhat_ref[i * SUB:(i + 1) * SUB, hl], khat_ref[i - 1, :, hl],
                                              NT_DIMS, preferred_element_type=F32))
            s_off = jnp.concatenate(blocks, axis=0).astype(BF16)
            intra = jnp.dot(s_off, vb_ref[:, hl], preferred_element_type=F32)
            obuf_ref[pl.ds(r0, CHUNK), hl] = inter + intra + od_ref[:, hl]
            upd = lax.dot_general(vb_ref[:, hl], kbar_ref[:, hl], TN_DIMS, preferred_element_type=F32)
            state_ref[hd * C_DIM:(hd + 1) * C_DIM, :] = st * jnp.exp(b_last[:, hl]) + upd
        return carry

    lax.fori_loop(0, C_TM // CHUNK, chunk_body, 0)

    gz = h_ref[:, 3 * D_MODEL:]
    gate = gz * _sigmoid(gz)
    for hd in range(C_HEADS):
        hl = slice(hd * C_DIM, (hd + 1) * C_DIM)
        oh = obuf_ref[:, hl]
        oh = oh * lax.rsqrt(jnp.mean(oh * oh, axis=-1, keepdims=True) + LN_EPS) * gn_ref[...]
        obuf_ref[:, hl] = oh * gate[:, hl]
    y = jnp.dot(obuf_ref[...].astype(BF16), wo_ref[...], preferred_element_type=F32)
    o_ref[...] = _layer_norm(ALPHA * x + y, g_ref[...], b_ref[...])


def _hgrn_layer(x, w_in, w_o, g_norm, gamma, g, b, batch, layer):
    t = x.shape[0]
    nt = t // batch // C_TM
    return pl.pallas_call(
        functools.partial(_hgrn_kernel, layer=layer),
        grid=(batch, nt),
        in_specs=[
            pl.BlockSpec((C_TM, D_MODEL), lambda bb, i: (bb * nt + i, 0)),
            _resident((D_MODEL, C_IN)),
            _resident((D_MODEL, D_MODEL)),
            _resident((1, C_DIM)),
            _resident((DEPTH, D_MODEL)),
            _resident((1, D_MODEL)),
            _resident((1, D_MODEL)),
        ],
        out_specs=pl.BlockSpec((C_TM, D_MODEL), lambda bb, i: (bb * nt + i, 0)),
        out_shape=jax.ShapeDtypeStruct((t, D_MODEL), F32),
        scratch_shapes=[
            pltpu.VMEM((C_TM, C_IN), F32),
            pltpu.VMEM((C_HEADS * C_DIM, C_DIM), F32),
            pltpu.VMEM((C_TM, D_MODEL), F32),
            pltpu.VMEM((CHUNK, D_MODEL), F32),
            pltpu.VMEM((CHUNK, D_MODEL), F32),
            pltpu.VMEM((CHUNK, D_MODEL), F32),
            pltpu.VMEM((CHUNK, D_MODEL), F32),
            pltpu.VMEM((CHUNK, D_MODEL), F32),
            pltpu.VMEM((CHUNK, D_MODEL), BF16),
            pltpu.VMEM((CHUNK, D_MODEL), BF16),
            pltpu.VMEM((CHUNK, D_MODEL), BF16),
            pltpu.VMEM((N_SUB - 1, CHUNK, D_MODEL), BF16),
            pltpu.VMEM((CHUNK, D_MODEL), BF16),
        ],
        compiler_params=pltpu.CompilerParams(
            dimension_semantics=("arbitrary", "arbitrary"), vmem_limit_bytes=VMEM_LIMIT),
        name="hgrn2_layer",
    )(x, w_in, w_o, g_norm, gamma, g, b)


def kernel(x, positions, ln_gain, ln_bias, ffn1_w_in, ffn1_w_out, ffn2_w_in, ffn2_w_out,
           ab_w_in, ab_w_o, diff_lambda, diff_subln, pool_w, pool_scale,
           c_w_in, c_w_o, c_norm, hgrn_gamma):
    batch, seq, d = x.shape
    t = batch * seq
    x = x.reshape(t, d)
    pos = positions.reshape(t, 1)
    inv_freq = ROPE_THETA ** (-jnp.arange(0, QK_DIM, 2, dtype=F32) / QK_DIM)
    freq = jnp.tile(inv_freq, HEAD_W // (QK_DIM // 2)).reshape(1, HEAD_W)

    def row(a):
        return a.reshape(1, -1)

    for l in range(DEPTH):
        j = l // 2
        x = _ffn_ln(x, ffn1_w_in[l].astype(BF16), ffn1_w_out[l].astype(BF16),
                    row(ln_gain[l, 0]), row(ln_bias[l, 0]))
        if l % 2 == 0:
            lambda_init = 0.8 - 0.6 * math.exp(-0.3 * l)
            w_ab = ab_w_in[j].astype(BF16)
            w_qku = jnp.concatenate([w_ab[:, :2 * A_WIDTH], w_ab[:, 3 * A_WIDTH:]], axis=1)
            w_vt = w_ab[:, 2 * A_WIDTH:3 * A_WIDTH].T
            qku, vt = _ab_in(x, pos, w_qku, w_vt, freq, pool_w[j].astype(BF16), row(pool_scale[j]), batch)
            x = _attn_out(qku, vt, x, ab_w_o[j].astype(BF16), diff_lambda[j], diff_subln[j].reshape(-1, 1),
                          row(ln_gain[l, 1]), row(ln_bias[l, 1]), batch, lambda_init)
        else:
            x = _hgrn_layer(x, c_w_in[j].astype(BF16), c_w_o[j].astype(BF16), row(c_norm[j]),
                            hgrn_gamma, row(ln_gain[l, 1]), row(ln_bias[l, 1]), batch, l)
        x = _ffn_ln(x, ffn2_w_in[l].astype(BF16), ffn2_w_out[l].astype(BF16),
                    row(ln_gain[l, 2]), row(ln_bias[l, 2]))
    return x.reshape(batch, seq, d)
```

```python
import functools
import math

import numpy as np
import jax
import jax.numpy as jnp
from jax import lax
from jax.experimental import pallas as pl
from jax.experimental.pallas import tpu as pltpu

D_MODEL = 1024
DEPTH = 4
QK_DIM = 64
HEAD_W = 2 * QK_DIM
A_WIDTH = 512
A_HEADS = 4
ROPE_THETA = 10000.0
POOL_WINDOWS = (2, 4, 8, 16)
POOL_HALO = 16
B_WIDTH = 512
POOL_GROUP = 128
AB_IN = 3 * A_WIDTH + B_WIDTH
QKU_W = 2 * A_WIDTH + B_WIDTH
C_DIM = 128
C_HEADS = 8
C_IN = 4 * D_MODEL
CHUNK = 64
SUB = 16
D_FF = 2816
ALPHA = (2 * DEPTH) ** 0.25
LN_EPS = 1e-5

VMEM_LIMIT = 56 * 1024 * 1024

F32 = jnp.float32
BF16 = jnp.bfloat16
NT_DIMS = (((1,), (1,)), ((), ()))
TN_DIMS = (((0,), (0,)), ((), ()))


def _resident(shape):
    return pl.BlockSpec(shape, lambda *_: (0,) * len(shape), pipeline_mode=pl.Buffered(1))


def _layer_slab(shape, layer):
    return pl.BlockSpec((None,) + tuple(shape), lambda *_: (layer,) + (0,) * len(shape),
                        pipeline_mode=pl.Buffered(1))


def _layer_norm(z, g, b):
    mu = jnp.mean(z, axis=-1, keepdims=True)
    d = z - mu
    var = jnp.mean(d * d, axis=-1, keepdims=True)
    return d * lax.rsqrt(var + LN_EPS) * g + b


def _sigmoid(x):
    return 1.0 / (1.0 + jnp.exp(-x))


FFN_TM = 512
FFN_SUB = 256


def _ffn_kernel(x_ref, win_ref, wout_ref, g_ref, b_ref, o_ref):
    def body(i, carry):
        r = pl.multiple_of(i * FFN_SUB, FFN_SUB)
        x = x_ref[pl.ds(r, FFN_SUB), :]
        h = jnp.dot(x.astype(BF16), win_ref[...], preferred_element_type=F32)
        gate = h[:, :D_FF]
        up = h[:, D_FF:]
        act = (gate * _sigmoid(gate) * up).astype(BF16)
        y = jnp.dot(act, wout_ref[...], preferred_element_type=F32)
        o_ref[pl.ds(r, FFN_SUB), :] = _layer_norm(ALPHA * x + 0.5 * y, g_ref[...], b_ref[...])
        return carry

    lax.fori_loop(0, FFN_TM // FFN_SUB, body, 0)


def _ffn_ln(x, w_in, w_out, layer, g, b):
    t = x.shape[0]
    return pl.pallas_call(
        _ffn_kernel,
        grid=(t // FFN_TM,),
        in_specs=[
            pl.BlockSpec((FFN_TM, D_MODEL), lambda i: (i, 0)),
            _layer_slab((D_MODEL, 2 * D_FF), layer),
            _layer_slab((D_FF, D_MODEL), layer),
            _resident((1, D_MODEL)),
            _resident((1, D_MODEL)),
        ],
        out_specs=pl.BlockSpec((FFN_TM, D_MODEL), lambda i: (i, 0)),
        out_shape=jax.ShapeDtypeStruct((t, D_MODEL), F32),
        compiler_params=pltpu.CompilerParams(
            dimension_semantics=("arbitrary",), vmem_limit_bytes=VMEM_LIMIT),
        name="ffn_ln",
    )(x, w_in, w_out, g, b)


AB_TM = 512
Q_SCALE = QK_DIM ** -0.5 * math.log2(math.e)


def _ab_in_kernel(x_ref, pos_ref, w_ref, wvt_ref, freq_ref, pw_ref, ps_ref, o_ref, vt_ref, ubuf_ref):
    i = pl.program_id(1)
    xb = x_ref[...].astype(BF16)
    h = jnp.dot(xb, w_ref[:, 0:2 * A_WIDTH], preferred_element_type=F32)
    u = jnp.dot(xb, w_ref[:, 3 * A_WIDTH:], preferred_element_type=F32)
    vt_ref[...] = lax.dot_general(wvt_ref[...], xb, NT_DIMS, preferred_element_type=F32).astype(BF16)

    ang = pos_ref[...].astype(F32) * freq_ref[...]
    lane = lax.broadcasted_iota(jnp.int32, (AB_TM, HEAD_W), 1)
    first_half = (lane % QK_DIM) < (QK_DIM // 2)
    cos = jnp.cos(ang)
    sin = jnp.sin(ang)
    sin = jnp.where(first_half, -sin, sin)

    def rope(t):
        partner = jnp.where(first_half,
                            pltpu.roll(t, HEAD_W - QK_DIM // 2, axis=1),
                            pltpu.roll(t, QK_DIM // 2, axis=1))
        return t * cos + partner * sin

    for hd in range(A_HEADS):
        sl = slice(hd * HEAD_W, (hd + 1) * HEAD_W)
        q = rope(h[:, sl]) * Q_SCALE
        o_ref[:, sl] = q.astype(BF16)
        ksl = slice(A_WIDTH + hd * HEAD_W, A_WIDTH + (hd + 1) * HEAD_W)
        o_ref[:, ksl] = rope(h[:, ksl]).astype(BF16)

    @pl.when(i == 0)
    def _():
        ubuf_ref[0:POOL_HALO, :] = jnp.zeros((POOL_HALO, B_WIDTH), F32)

    ubuf_ref[POOL_HALO:, :] = u
    seq_idx = i * AB_TM + lax.broadcasted_iota(jnp.int32, (AB_TM, 1), 0)
    count = (seq_idx + 1).astype(F32)
    for g, w in enumerate(POOL_WINDOWS):
        sl = slice(g * POOL_GROUP, (g + 1) * POOL_GROUP)
        acc = ubuf_ref[:, sl]
        step = 1
        while step < w:
            acc = acc + pltpu.roll(acc, step, axis=0)
            step *= 2
        mean = acc[POOL_HALO:, :] / jnp.minimum(count, float(w))
        d = (mean - u[:, sl]).astype(BF16)
        y = jnp.dot(d, pw_ref[g], preferred_element_type=F32) * ps_ref[:, sl]
        o_ref[:, 2 * A_WIDTH + g * POOL_GROUP:2 * A_WIDTH + (g + 1) * POOL_GROUP] = y.astype(BF16)
    ubuf_ref[0:POOL_HALO, :] = u[AB_TM - POOL_HALO:, :]


def _ab_in(x, pos, w_in, w_vt, layer, freq, pool_w, pool_scale, batch):
    t = x.shape[0]
    seq = t // batch
    nt = seq // AB_TM
    return pl.pallas_call(
        _ab_in_kernel,
        grid=(batch, nt),
        in_specs=[
            pl.BlockSpec((AB_TM, D_MODEL), lambda b, i: (b * nt + i, 0)),
            pl.BlockSpec((AB_TM, 1), lambda b, i: (b * nt + i, 0)),
            _layer_slab((D_MODEL, AB_IN), layer),
            _layer_slab((A_WIDTH, D_MODEL), layer),
            _resident((1, HEAD_W)),
            _layer_slab((len(POOL_WINDOWS), POOL_GROUP, POOL_GROUP), layer),
            _resident((1, B_WIDTH)),
        ],
        out_specs=[
            pl.BlockSpec((AB_TM, QKU_W), lambda b, i: (b * nt + i, 0)),
            pl.BlockSpec((None, A_WIDTH, AB_TM), lambda b, i: (b, 0, i)),
        ],
        out_shape=[
            jax.ShapeDtypeStruct((t, QKU_W), BF16),
            jax.ShapeDtypeStruct((batch, A_WIDTH, seq), BF16),
        ],
        scratch_shapes=[pltpu.VMEM((POOL_HALO + AB_TM, B_WIDTH), F32)],
        compiler_params=pltpu.CompilerParams(
            dimension_semantics=("arbitrary", "arbitrary"), vmem_limit_bytes=VMEM_LIMIT),
        name="ab_in",
    )(x, pos, w_in, w_vt, freq, pool_w, pool_scale)


ATT_TQ = 512
ATT_TK = 512
N_MAPS = 2 * A_HEADS


def _attn_kernel(q_ref, k_ref, vt_ref, ob_ref, x_ref, wo_ref, lam_ref, sub_ref, g_ref, b_ref,
                 o_ref, acc_ref, oat_ref, qz_ref, s_ref, m_ref, l_ref, *, lambda_init):
    qi = pl.program_id(1)
    lv = lam_ref[...]
    lam = (jnp.exp(jnp.sum(lv[0:1] * lv[1:2], axis=-1, keepdims=True))
           - jnp.exp(jnp.sum(lv[2:3] * lv[3:4], axis=-1, keepdims=True)) + lambda_init)

    lane = lax.broadcasted_iota(jnp.int32, (ATT_TQ, HEAD_W), 1)
    key_idx = lax.broadcasted_iota(jnp.int32, (ATT_TK, ATT_TQ), 0)
    qry_idx = lax.broadcasted_iota(jnp.int32, (ATT_TK, ATT_TQ), 1)

    heads = [slice(hd * HEAD_W, (hd + 1) * HEAD_W) for hd in range(A_HEADS)]
    for hd, sl in enumerate(heads):
        qh = q_ref[:, sl]
        qz_ref[2 * hd] = jnp.where(lane < QK_DIM, qh, jnp.zeros_like(qh))
        qz_ref[2 * hd + 1] = jnp.where(lane >= QK_DIM, qh, jnp.zeros_like(qh))
    acc_ref[...] = jnp.zeros(acc_ref.shape, F32)
    m_ref[...] = jnp.full(m_ref.shape, -jnp.inf, F32)
    l_ref[...] = jnp.zeros(l_ref.shape, F32)

    last_block = jnp.maximum(qi - 1, 0)

    def scores(buf, j, i):
        r = pl.multiple_of(j * ATT_TK, ATT_TK)
        kb = k_ref[pl.ds(r, ATT_TK), heads[i // 2]]
        s_ref[buf, i] = lax.dot_general(kb, qz_ref[i], NT_DIMS, preferred_element_type=F32)

    def consume(buf, j, i, masked):
        sc = s_ref[buf, i]
        if masked:
            sc = jnp.where(key_idx <= qry_idx, sc, -jnp.inf)
        m_old = m_ref[i:i + 1, :]
        m_new = jnp.maximum(m_old, jnp.max(sc, axis=0, keepdims=True))
        scale = jnp.exp2(m_old - m_new)
        p = jnp.exp2(sc - m_new)
        l_ref[i:i + 1, :] = scale * l_ref[i:i + 1, :] + jnp.sum(p, axis=0, keepdims=True)
        m_ref[i:i + 1, :] = m_new
        r = pl.multiple_of(j * ATT_TK, ATT_TK)
        vtb = vt_ref[heads[i // 2], pl.ds(r, ATT_TK)]
        acc_ref[i] = scale * acc_ref[i] + jnp.dot(vtb, p.astype(BF16), preferred_element_type=F32)

    def visit(buf, j, masked, prefetch):
        for i in range(N_MAPS):
            if prefetch is not None:
                scores(1 - buf, prefetch, i)
            consume(buf, j, i, masked)

    def pair(t, carry):
        visit(1, 2 * t, False, jnp.minimum(2 * t + 1, last_block))
        visit(0, 2 * t + 1, False, jnp.minimum(2 * t + 2, last_block))
        return carry

    for i in range(N_MAPS):
        scores(0, qi, i)
    visit(0, qi, True, 0)
    lax.fori_loop(0, qi // 2, pair, 0)

    @pl.when(qi % 2 == 1)
    def _():
        visit(1, qi - 1, False, None)

    for hd, sl in enumerate(heads):
        o = (acc_ref[2 * hd] / l_ref[2 * hd:2 * hd + 1, :]
             - lam * (acc_ref[2 * hd + 1] / l_ref[2 * hd + 1:2 * hd + 2, :]))
        o = o * lax.rsqrt(jnp.mean(o * o, axis=0, keepdims=True) + LN_EPS) * sub_ref[...]
        oat_ref[sl, :] = (o * (1.0 - lambda_init)).astype(BF16)

    y = lax.dot_general(oat_ref[...], wo_ref[0:A_WIDTH, :], TN_DIMS, preferred_element_type=F32)
    y = y + jnp.dot(ob_ref[...], wo_ref[A_WIDTH:, :], preferred_element_type=F32)
    o_ref[...] = _layer_norm(ALPHA * x_ref[...] + y, g_ref[...], b_ref[...])


def _attn_out(qku, vt, x, w_o, layer, lam_vecs, subln, g, b, batch, lambda_init):
    t = x.shape[0]
    seq = t // batch
    nq = seq // ATT_TQ
    return pl.pallas_call(
        functools.partial(_attn_kernel, lambda_init=lambda_init),
        grid=(batch, nq),
        in_specs=[
            pl.BlockSpec((ATT_TQ, A_WIDTH), lambda bb, i: (bb * nq + i, 0)),
            pl.BlockSpec((seq, A_WIDTH), lambda bb, i: (bb, 1), pipeline_mode=pl.Buffered(1)),
            pl.BlockSpec((None, A_WIDTH, seq), lambda bb, i: (bb, 0, 0), pipeline_mode=pl.Buffered(1)),
            pl.BlockSpec((ATT_TQ, B_WIDTH), lambda bb, i: (bb * nq + i, 2)),
            pl.BlockSpec((ATT_TQ, D_MODEL), lambda bb, i: (bb * nq + i, 0)),
            _layer_slab((D_MODEL, D_MODEL), layer),
            _resident((4, QK_DIM)),
            _resident((HEAD_W, 1)),
            _resident((1, D_MODEL)),
            _resident((1, D_MODEL)),
        ],
        out_specs=pl.BlockSpec((ATT_TQ, D_MODEL), lambda bb, i: (bb * nq + i, 0)),
        out_shape=jax.ShapeDtypeStruct((t, D_MODEL), F32),
        scratch_shapes=[
            pltpu.VMEM((N_MAPS, HEAD_W, ATT_TQ), F32),
            pltpu.VMEM((A_WIDTH, ATT_TQ), BF16),
            pltpu.VMEM((N_MAPS, ATT_TQ, HEAD_W), BF16),
            pltpu.VMEM((2, N_MAPS, ATT_TK, ATT_TQ), F32),
            pltpu.VMEM((N_MAPS, ATT_TQ), F32),
            pltpu.VMEM((N_MAPS, ATT_TQ), F32),
        ],
        compiler_params=pltpu.CompilerParams(
            dimension_semantics=("arbitrary", "arbitrary"), vmem_limit_bytes=VMEM_LIMIT),
        name="diff_attn_out",
    )(qku, qku, vt, qku, x, w_o, lam_vecs, subln, g, b)


C_TM = 256
N_SUB = CHUNK // SUB
HALF = SUB // 2
MASKED_EXPONENT = -1e30


def _hgrn_kernel(x_ref, win_ref, wo_ref, gn_ref, gamma_ref, g_ref, b_ref, o_ref,
                 h_ref, state_ref, obuf_ref, q_ref, b2_ref, c_ref, wd_ref,
                 qt_ref, kbar_ref, qhat_ref, khat_ref, vb_ref, *, layer):
    @pl.when(pl.program_id(1) == 0)
    def _():
        state_ref[...] = jnp.zeros(state_ref.shape, F32)

    gam = gamma_ref[...]
    e = jnp.exp(gam - jnp.max(gam, axis=0, keepdims=True))
    p = e / jnp.sum(e, axis=0, keepdims=True)
    lb = jnp.sum(p[1:layer + 1], axis=0, keepdims=True)

    x = x_ref[...]
    h_ref[...] = jnp.dot(x.astype(BF16), win_ref[...], preferred_element_type=F32)

    tri = (lax.broadcasted_iota(jnp.int32, (CHUNK, CHUNK), 1)
           <= lax.broadcasted_iota(jnp.int32, (CHUNK, CHUNK), 0)).astype(BF16)
    row8 = lax.broadcasted_iota(jnp.int32, (HALF, 1), 0)
    lane_s = lax.broadcasted_iota(jnp.int32, (HALF, CHUNK), 1)

    def chunk_body(c, carry):
        r0 = pl.multiple_of(c * CHUNK, CHUNK)
        hq = h_ref[pl.ds(r0, CHUNK), 0:D_MODEL]
        q = hq * _sigmoid(hq)
        f = lb + (1.0 - lb) * _sigmoid(h_ref[pl.ds(r0, CHUNK), D_MODEL:2 * D_MODEL])
        kk = 1.0 - f
        iv = h_ref[pl.ds(r0, CHUNK), 2 * D_MODEL:3 * D_MODEL]
        log2f = jnp.log2(f)
        b2 = jnp.zeros((CHUNK, D_MODEL), F32)
        for _ in range(3):
            term = log2f.astype(BF16)
            b2 = b2 + jnp.dot(tri, term, preferred_element_type=F32)
            log2f = log2f - term.astype(F32)
        b2_last = b2[CHUNK - 1:CHUNK, :]
        q_ref[...] = q
        b2_ref[...] = b2
        c_ref[...] = b2 - jnp.log2(kk)
        qt_ref[...] = (q * jnp.exp2(b2)).astype(BF16)
        kbar_ref[...] = (kk * jnp.exp2(b2_last - b2)).astype(BF16)
        vb_ref[...] = iv.astype(BF16)

        for i in range(1, N_SUB):
            rows = slice(i * SUB, (i + 1) * SUB)
            ref_i = b2[i * SUB - 1:i * SUB, :]
            qhat_ref[rows, :] = (q[rows] * jnp.exp2(b2[rows] - ref_i)).astype(BF16)
            prev = slice(0, i * SUB)
            khat_ref[i - 1, prev, :] = (kk[prev] * jnp.exp2(ref_i - b2[prev])).astype(BF16)
            khat_ref[i - 1, i * SUB:, :] = jnp.zeros((CHUNK - i * SUB, D_MODEL), BF16)

        for hd in range(C_HEADS):
            hl = slice(hd * C_DIM, (hd + 1) * C_DIM)
            st = state_ref[hd * C_DIM:(hd + 1) * C_DIM, :]
            obuf_ref[pl.ds(r0, CHUNK), hl] = lax.dot_general(
                qt_ref[:, hl], st.astype(BF16), NT_DIMS, preferred_element_type=F32)
            wd_ref[hd, 0:SUB, :] = jnp.zeros((SUB, CHUNK), F32)
            for i in range(1, N_SUB):
                wd_ref[hd, i * SUB:(i + 1) * SUB, :] = lax.dot_general(
                    qhat_ref[i * SUB:(i + 1) * SUB, hl], khat_ref[i - 1, :, hl],
                    NT_DIMS, preferred_element_type=F32)
            upd = lax.dot_general(vb_ref[:, hl], kbar_ref[:, hl], TN_DIMS, preferred_element_type=F32)
            state_ref[hd * C_DIM:(hd + 1) * C_DIM, :] = st * jnp.exp2(b2_last[:, hl]) + upd

        for i in range(N_SUB):
            a = i * SUB
            q_half = (q_ref[a:a + HALF, :], q_ref[a + HALF:a + SUB, :])
            b_half = (b2_ref[a:a + HALF, :], b2_ref[a + HALF:a + SUB, :])
            w = [[jnp.zeros((HALF, CHUNK), F32) for _ in range(C_HEADS)] for _ in range(2)]
            for s in range(SUB):
                c_row = jnp.broadcast_to(c_ref[a + s:a + s + 1, :], (HALF, D_MODEL))
                for half in range(2):
                    first_row = s - half * HALF
                    if first_row >= HALF:
                        continue
                    arg = b_half[half] - c_row
                    if first_row > 0:
                        arg = jnp.where(row8 >= first_row, arg, MASKED_EXPONENT)
                    e_ts = q_half[half] * jnp.exp2(arg)
                    for hd in range(C_HEADS):
                        w_ts = jnp.sum(e_ts[:, hd * C_DIM:(hd + 1) * C_DIM], axis=-1, keepdims=True)
                        w[half][hd] = jnp.where(lane_s == a + s, w_ts, w[half][hd])
            for hd in range(C_HEADS):
                wd_ref[hd, a:a + HALF, :] += w[0][hd]
                wd_ref[hd, a + HALF:a + SUB, :] += w[1][hd]

        for hd in range(C_HEADS):
            hl = slice(hd * C_DIM, (hd + 1) * C_DIM)
            intra = jnp.dot(wd_ref[hd].astype(BF16), vb_ref[:, hl], preferred_element_type=F32)
            obuf_ref[pl.ds(r0, CHUNK), hl] += intra
        return carry

    lax.fori_loop(0, C_TM // CHUNK, chunk_body, 0)

    gz = h_ref[:, 3 * D_MODEL:]
    gate = gz * _sigmoid(gz)
    for hd in range(C_HEADS):
        hl = slice(hd * C_DIM, (hd + 1) * C_DIM)
        oh = obuf_ref[:, hl]
        oh = oh * lax.rsqrt(jnp.mean(oh * oh, axis=-1, keepdims=True) + LN_EPS) * gn_ref[...]
        obuf_ref[:, hl] = oh * gate[:, hl]
    y = jnp.dot(obuf_ref[...].astype(BF16), wo_ref[...], preferred_element_type=F32)
    o_ref[...] = _layer_norm(ALPHA * x + y, g_ref[...], b_ref[...])


def _hgrn_layer(x, w_in, w_o, slab, g_norm, gamma, g, b, batch, layer):
    t = x.shape[0]
    nt = t // batch // C_TM
    return pl.pallas_call(
        functools.partial(_hgrn_kernel, layer=layer),
        grid=(batch, nt),
        in_specs=[
            pl.BlockSpec((C_TM, D_MODEL), lambda bb, i: (bb * nt + i, 0)),
            _layer_slab((D_MODEL, C_IN), slab),
            _layer_slab((D_MODEL, D_MODEL), slab),
            _resident((1, C_DIM)),
            _resident((DEPTH, D_MODEL)),
            _resident((1, D_MODEL)),
            _resident((1, D_MODEL)),
        ],
        out_specs=pl.BlockSpec((C_TM, D_MODEL), lambda bb, i: (bb * nt + i, 0)),
        out_shape=jax.ShapeDtypeStruct((t, D_MODEL), F32),
        scratch_shapes=[
            pltpu.VMEM((C_TM, C_IN), F32),
            pltpu.VMEM((C_HEADS * C_DIM, C_DIM), F32),
            pltpu.VMEM((C_TM, D_MODEL), F32),
            pltpu.VMEM((CHUNK, D_MODEL), F32),
            pltpu.VMEM((CHUNK, D_MODEL), F32),
            pltpu.VMEM((CHUNK, D_MODEL), F32),
            pltpu.VMEM((C_HEADS, CHUNK, CHUNK), F32),
            pltpu.VMEM((CHUNK, D_MODEL), BF16),
            pltpu.VMEM((CHUNK, D_MODEL), BF16),
            pltpu.VMEM((CHUNK, D_MODEL), BF16),
            pltpu.VMEM((N_SUB - 1, CHUNK, D_MODEL), BF16),
            pltpu.VMEM((CHUNK, D_MODEL), BF16),
        ],
        compiler_params=pltpu.CompilerParams(
            dimension_semantics=("arbitrary", "arbitrary"), vmem_limit_bytes=VMEM_LIMIT),
        name="hgrn2_layer",
    )(x, w_in, w_o, g_norm, gamma, g, b)


def kernel(x, positions, ln_gain, ln_bias, ffn1_w_in, ffn1_w_out, ffn2_w_in, ffn2_w_out,
           ab_w_in, ab_w_o, diff_lambda, diff_subln, pool_w, pool_scale,
           c_w_in, c_w_o, c_norm, hgrn_gamma):
    batch, seq, d = x.shape
    t = batch * seq
    x = x.reshape(t, d)
    pos = positions.reshape(t, 1)
    inv_freq = ROPE_THETA ** (-jnp.arange(0, QK_DIM, 2, dtype=F32) / QK_DIM)
    freq = jnp.tile(inv_freq, HEAD_W // (QK_DIM // 2)).reshape(1, HEAD_W)

    def row(a):
        return a.reshape(1, -1)

    ffn1_in, ffn1_out = ffn1_w_in.astype(BF16), ffn1_w_out.astype(BF16)
    ffn2_in, ffn2_out = ffn2_w_in.astype(BF16), ffn2_w_out.astype(BF16)
    ab_in, ab_out = ab_w_in.astype(BF16), ab_w_o.astype(BF16)
    ab_vt = jnp.swapaxes(ab_in[:, :, 2 * A_WIDTH:3 * A_WIDTH], 1, 2)
    pool_wb = pool_w.astype(BF16)
    c_in, c_out = c_w_in.astype(BF16), c_w_o.astype(BF16)

    for l in range(DEPTH):
        j = l // 2
        x = _ffn_ln(x, ffn1_in, ffn1_out, l, row(ln_gain[l, 0]), row(ln_bias[l, 0]))
        if l % 2 == 0:
            lambda_init = 0.8 - 0.6 * math.exp(-0.3 * l)
            qku, vt = _ab_in(x, pos, ab_in, ab_vt, j, freq, pool_wb, row(pool_scale[j]), batch)
            x = _attn_out(qku, vt, x, ab_out, j, diff_lambda[j], diff_subln[j].reshape(-1, 1),
                          row(ln_gain[l, 1]), row(ln_bias[l, 1]), batch, lambda_init)
        else:
            x = _hgrn_layer(x, c_in, c_out, j, row(c_norm[j]),
                            hgrn_gamma, row(ln_gain[l, 1]), row(ln_bias[l, 1]), batch, l)
        x = _ffn_ln(x, ffn2_in, ffn2_out, l, row(ln_gain[l, 2]), row(ln_bias[l, 2]))
    return x.reshape(batch, seq, d)
```

```python
import functools
import math

import numpy as np
import jax
import jax.numpy as jnp
from jax import lax
from jax.experimental import pallas as pl
from jax.experimental.pallas import tpu as pltpu

D_MODEL = 1024
DEPTH = 4
QK_DIM = 64
HEAD_W = 2 * QK_DIM
A_WIDTH = 512
A_HEADS = 4
ROPE_THETA = 10000.0
POOL_WINDOWS = (2, 4, 8, 16)
POOL_HALO = 16
B_WIDTH = 512
POOL_GROUP = 128
AB_IN = 3 * A_WIDTH + B_WIDTH
QKU_W = 2 * A_WIDTH + B_WIDTH
C_DIM = 128
C_HEADS = 8
C_IN = 4 * D_MODEL
CHUNK = 64
SUB = 16
D_FF = 2816
ALPHA = (2 * DEPTH) ** 0.25
LN_EPS = 1e-5

VMEM_LIMIT = 56 * 1024 * 1024

F32 = jnp.float32
BF16 = jnp.bfloat16
NT_DIMS = (((1,), (1,)), ((), ()))
TN_DIMS = (((0,), (0,)), ((), ()))


def _resident(shape):
    return pl.BlockSpec(shape, lambda *_: (0,) * len(shape), pipeline_mode=pl.Buffered(1))


def _layer_slab(shape, layer):
    return pl.BlockSpec((None,) + tuple(shape), lambda *_: (layer,) + (0,) * len(shape),
                        pipeline_mode=pl.Buffered(1))


def _layer_norm(z, g, b):
    mu = jnp.mean(z, axis=-1, keepdims=True)
    d = z - mu
    var = jnp.mean(d * d, axis=-1, keepdims=True)
    return d * lax.rsqrt(var + LN_EPS) * g + b


def _sigmoid(x):
    return 1.0 / (1.0 + jnp.exp(-x))


FFN_TM = 1024
FFN_SUB = 256


def _ffn_kernel(x_ref, win_ref, wout_ref, g_ref, b_ref, o_ref, y_ref):
    n_sub = FFN_TM // FFN_SUB
    for s in range(n_sub + 1):
        if s < n_sub:
            x = x_ref[s * FFN_SUB:(s + 1) * FFN_SUB, :]
            h = jnp.dot(x.astype(BF16), win_ref[...], preferred_element_type=F32)
            gate = h[:, :D_FF]
            up = h[:, D_FF:]
            act = (gate * _sigmoid(gate) * up).astype(BF16)
            y = jnp.dot(act, wout_ref[...], preferred_element_type=F32)
        if s > 0:
            prev = slice((s - 1) * FFN_SUB, s * FFN_SUB)
            o_ref[prev, :] = _layer_norm(ALPHA * x_ref[prev, :] + 0.5 * y_ref[...], g_ref[...], b_ref[...])
        if s < n_sub:
            y_ref[...] = y


def _ffn_ln(x, w_in, w_out, layer, g, b):
    t = x.shape[0]
    return pl.pallas_call(
        _ffn_kernel,
        grid=(t // FFN_TM,),
        in_specs=[
            pl.BlockSpec((FFN_TM, D_MODEL), lambda i: (i, 0)),
            _layer_slab((D_MODEL, 2 * D_FF), layer),
            _layer_slab((D_FF, D_MODEL), layer),
            _resident((1, D_MODEL)),
            _resident((1, D_MODEL)),
        ],
        out_specs=pl.BlockSpec((FFN_TM, D_MODEL), lambda i: (i, 0)),
        out_shape=jax.ShapeDtypeStruct((t, D_MODEL), F32),
        scratch_shapes=[pltpu.VMEM((FFN_SUB, D_MODEL), F32)],
        compiler_params=pltpu.CompilerParams(
            dimension_semantics=("arbitrary",), vmem_limit_bytes=VMEM_LIMIT),
        name="ffn_ln",
    )(x, w_in, w_out, g, b)


AB_TM = 512
Q_SCALE = QK_DIM ** -0.5 * math.log2(math.e)


def _ab_in_kernel(x_ref, pos_ref, w_ref, wvt_ref, freq_ref, pw_ref, ps_ref, o_ref, vt_ref, ubuf_ref):
    i = pl.program_id(1)
    xb = x_ref[...].astype(BF16)
    h = jnp.dot(xb, w_ref[:, 0:2 * A_WIDTH], preferred_element_type=F32)
    u = jnp.dot(xb, w_ref[:, 3 * A_WIDTH:], preferred_element_type=F32)
    vt_ref[...] = lax.dot_general(wvt_ref[...], xb, NT_DIMS, preferred_element_type=F32).astype(BF16)

    ang = pos_ref[...].astype(F32) * freq_ref[...]
    lane = lax.broadcasted_iota(jnp.int32, (AB_TM, HEAD_W), 1)
    first_half = (lane % QK_DIM) < (QK_DIM // 2)
    cos = jnp.cos(ang)
    sin = jnp.sin(ang)
    sin = jnp.where(first_half, -sin, sin)

    def rope(t):
        partner = jnp.where(first_half,
                            pltpu.roll(t, HEAD_W - QK_DIM // 2, axis=1),
                            pltpu.roll(t, QK_DIM // 2, axis=1))
        return t * cos + partner * sin

    for hd in range(A_HEADS):
        sl = slice(hd * HEAD_W, (hd + 1) * HEAD_W)
        q = rope(h[:, sl]) * Q_SCALE
        o_ref[:, sl] = q.astype(BF16)
        ksl = slice(A_WIDTH + hd * HEAD_W, A_WIDTH + (hd + 1) * HEAD_W)
        o_ref[:, ksl] = rope(h[:, ksl]).astype(BF16)

    @pl.when(i == 0)
    def _():
        ubuf_ref[0:POOL_HALO, :] = jnp.zeros((POOL_HALO, B_WIDTH), F32)

    ubuf_ref[POOL_HALO:, :] = u
    seq_idx = i * AB_TM + lax.broadcasted_iota(jnp.int32, (AB_TM, 1), 0)
    count = (seq_idx + 1).astype(F32)
    for g, w in enumerate(POOL_WINDOWS):
        sl = slice(g * POOL_GROUP, (g + 1) * POOL_GROUP)
        acc = ubuf_ref[:, sl]
        step = 1
        while step < w:
            acc = acc + pltpu.roll(acc, step, axis=0)
            step *= 2
        mean = acc[POOL_HALO:, :] / jnp.minimum(count, float(w))
        d = (mean - u[:, sl]).astype(BF16)
        y = jnp.dot(d, pw_ref[g], preferred_element_type=F32) * ps_ref[:, sl]
        o_ref[:, 2 * A_WIDTH + g * POOL_GROUP:2 * A_WIDTH + (g + 1) * POOL_GROUP] = y.astype(BF16)
    ubuf_ref[0:POOL_HALO, :] = u[AB_TM - POOL_HALO:, :]


def _ab_in(x, pos, w_in, w_vt, layer, freq, pool_w, pool_scale, batch):
    t = x.shape[0]
    seq = t // batch
    nt = seq // AB_TM
    return pl.pallas_call(
        _ab_in_kernel,
        grid=(batch, nt),
        in_specs=[
            pl.BlockSpec((AB_TM, D_MODEL), lambda b, i: (b * nt + i, 0)),
            pl.BlockSpec((AB_TM, 1), lambda b, i: (b * nt + i, 0)),
            _layer_slab((D_MODEL, AB_IN), layer),
            _layer_slab((A_WIDTH, D_MODEL), layer),
            _resident((1, HEAD_W)),
            _layer_slab((len(POOL_WINDOWS), POOL_GROUP, POOL_GROUP), layer),
            _resident((1, B_WIDTH)),
        ],
        out_specs=[
            pl.BlockSpec((AB_TM, QKU_W), lambda b, i: (b * nt + i, 0)),
            pl.BlockSpec((None, A_WIDTH, AB_TM), lambda b, i: (b, 0, i)),
        ],
        out_shape=[
            jax.ShapeDtypeStruct((t, QKU_W), BF16),
            jax.ShapeDtypeStruct((batch, A_WIDTH, seq), BF16),
        ],
        scratch_shapes=[pltpu.VMEM((POOL_HALO + AB_TM, B_WIDTH), F32)],
        compiler_params=pltpu.CompilerParams(
            dimension_semantics=("arbitrary", "arbitrary"), vmem_limit_bytes=VMEM_LIMIT),
        name="ab_in",
    )(x, pos, w_in, w_vt, freq, pool_w, pool_scale)


ATT_TQ = 512
ATT_TK = 512
N_MAPS = 2 * A_HEADS


def _attn_kernel(q_ref, k_ref, vt_ref, ob_ref, x_ref, wo_ref, lam_ref, sub_ref, g_ref, b_ref,
                 o_ref, acc_ref, oat_ref, qz_ref, s_ref, m_ref, l_ref, *, lambda_init):
    qi = pl.program_id(1)
    lv = lam_ref[...]
    lam = (jnp.exp(jnp.sum(lv[0:1] * lv[1:2], axis=-1, keepdims=True))
           - jnp.exp(jnp.sum(lv[2:3] * lv[3:4], axis=-1, keepdims=True)) + lambda_init)

    lane = lax.broadcasted_iota(jnp.int32, (ATT_TQ, HEAD_W), 1)
    key_idx = lax.broadcasted_iota(jnp.int32, (ATT_TK, ATT_TQ), 0)
    qry_idx = lax.broadcasted_iota(jnp.int32, (ATT_TK, ATT_TQ), 1)

    heads = [slice(hd * HEAD_W, (hd + 1) * HEAD_W) for hd in range(A_HEADS)]
    for hd, sl in enumerate(heads):
        qh = q_ref[:, sl]
        qz_ref[2 * hd] = jnp.where(lane < QK_DIM, qh, jnp.zeros_like(qh))
        qz_ref[2 * hd + 1] = jnp.where(lane >= QK_DIM, qh, jnp.zeros_like(qh))
    acc_ref[...] = jnp.zeros(acc_ref.shape, F32)
    m_ref[...] = jnp.full(m_ref.shape, -jnp.inf, F32)
    l_ref[...] = jnp.zeros(l_ref.shape, F32)

    last_block = jnp.maximum(qi - 1, 0)

    def scores(buf, j, i):
        r = pl.multiple_of(j * ATT_TK, ATT_TK)
        kb = k_ref[pl.ds(r, ATT_TK), heads[i // 2]]
        s_ref[buf, i] = lax.dot_general(kb, qz_ref[i], NT_DIMS, preferred_element_type=F32)

    def consume(buf, j, i, masked):
        sc = s_ref[buf, i]
        if masked:
            sc = jnp.where(key_idx <= qry_idx, sc, -jnp.inf)
        m_old = m_ref[i:i + 1, :]
        m_new = jnp.maximum(m_old, jnp.max(sc, axis=0, keepdims=True))
        scale = jnp.exp2(m_old - m_new)
        p = jnp.exp2(sc - m_new)
        l_ref[i:i + 1, :] = scale * l_ref[i:i + 1, :] + jnp.sum(p, axis=0, keepdims=True)
        m_ref[i:i + 1, :] = m_new
        r = pl.multiple_of(j * ATT_TK, ATT_TK)
        vtb = vt_ref[heads[i // 2], pl.ds(r, ATT_TK)]
        acc_ref[i] = scale * acc_ref[i] + jnp.dot(vtb, p.astype(BF16), preferred_element_type=F32)

    def visit(buf, j, masked, prefetch):
        for i in range(N_MAPS):
            if prefetch is not None:
                scores(1 - buf, prefetch, i)
            consume(buf, j, i, masked)

    def pair(t, carry):
        visit(1, 2 * t, False, jnp.minimum(2 * t + 1, last_block))
        visit(0, 2 * t + 1, False, jnp.minimum(2 * t + 2, last_block))
        return carry

    for i in range(N_MAPS):
        scores(0, qi, i)
    visit(0, qi, True, 0)
    lax.fori_loop(0, qi // 2, pair, 0)

    @pl.when(qi % 2 == 1)
    def _():
        visit(1, qi - 1, False, None)

    for hd, sl in enumerate(heads):
        o = (acc_ref[2 * hd] / l_ref[2 * hd:2 * hd + 1, :]
             - lam * (acc_ref[2 * hd + 1] / l_ref[2 * hd + 1:2 * hd + 2, :]))
        o = o * lax.rsqrt(jnp.mean(o * o, axis=0, keepdims=True) + LN_EPS) * sub_ref[...]
        oat_ref[sl, :] = (o * (1.0 - lambda_init)).astype(BF16)

    y = lax.dot_general(oat_ref[...], wo_ref[0:A_WIDTH, :], TN_DIMS, preferred_element_type=F32)
    y = y + jnp.dot(ob_ref[...], wo_ref[A_WIDTH:, :], preferred_element_type=F32)
    o_ref[...] = _layer_norm(ALPHA * x_ref[...] + y, g_ref[...], b_ref[...])


def _attn_out(qku, vt, x, w_o, layer, lam_vecs, subln, g, b, batch, lambda_init):
    t = x.shape[0]
    seq = t // batch
    nq = seq // ATT_TQ
    return pl.pallas_call(
        functools.partial(_attn_kernel, lambda_init=lambda_init),
        grid=(batch, nq),
        in_specs=[
            pl.BlockSpec((ATT_TQ, A_WIDTH), lambda bb, i: (bb * nq + i, 0)),
            pl.BlockSpec((seq, A_WIDTH), lambda bb, i: (bb, 1), pipeline_mode=pl.Buffered(1)),
            pl.BlockSpec((None, A_WIDTH, seq), lambda bb, i: (bb, 0, 0), pipeline_mode=pl.Buffered(1)),
            pl.BlockSpec((ATT_TQ, B_WIDTH), lambda bb, i: (bb * nq + i, 2)),
            pl.BlockSpec((ATT_TQ, D_MODEL), lambda bb, i: (bb * nq + i, 0)),
            _layer_slab((D_MODEL, D_MODEL), layer),
            _resident((4, QK_DIM)),
            _resident((HEAD_W, 1)),
            _resident((1, D_MODEL)),
            _resident((1, D_MODEL)),
        ],
        out_specs=pl.BlockSpec((ATT_TQ, D_MODEL), lambda bb, i: (bb * nq + i, 0)),
        out_shape=jax.ShapeDtypeStruct((t, D_MODEL), F32),
        scratch_shapes=[
            pltpu.VMEM((N_MAPS, HEAD_W, ATT_TQ), F32),
            pltpu.VMEM((A_WIDTH, ATT_TQ), BF16),
            pltpu.VMEM((N_MAPS, ATT_TQ, HEAD_W), BF16),
            pltpu.VMEM((2, N_MAPS, ATT_TK, ATT_TQ), F32),
            pltpu.VMEM((N_MAPS, ATT_TQ), F32),
            pltpu.VMEM((N_MAPS, ATT_TQ), F32),
        ],
        compiler_params=pltpu.CompilerParams(
            dimension_semantics=("arbitrary", "arbitrary"), vmem_limit_bytes=VMEM_LIMIT),
        name="diff_attn_out",
    )(qku, qku, vt, qku, x, w_o, lam_vecs, subln, g, b)


C_TM = 512
C_PIECE = 128
N_PROJ = C_IN // D_MODEL
N_SUB = CHUNK // SUB
HALF = SUB // 2
MASKED_EXPONENT = -1e30


def _hgrn_kernel(x_ref, win_ref, wo_ref, gn_ref, gamma_ref, g_ref, b_ref, o_ref,
                 h_ref, state_ref, obuf_ref, q_ref, b2_ref, c_ref, wd_ref,
                 qt_ref, kbar_ref, qhat_ref, khat_ref, vb_ref, *, layer):
    @pl.when(pl.program_id(1) == 0)
    def _():
        state_ref[...] = jnp.zeros(state_ref.shape, F32)

    gam = gamma_ref[...]
    e = jnp.exp(gam - jnp.max(gam, axis=0, keepdims=True))
    p = e / jnp.sum(e, axis=0, keepdims=True)
    lb = jnp.sum(p[1:layer + 1], axis=0, keepdims=True)

    def project(piece, group):
        rows = slice(piece * C_PIECE, (piece + 1) * C_PIECE)
        cols = slice(group * D_MODEL, (group + 1) * D_MODEL)
        h_ref[rows, cols] = jnp.dot(x_ref[rows, :].astype(BF16), win_ref[:, cols], preferred_element_type=F32)

    tri = (lax.broadcasted_iota(jnp.int32, (CHUNK, CHUNK), 1)
           <= lax.broadcasted_iota(jnp.int32, (CHUNK, CHUNK), 0)).astype(BF16)
    row8 = lax.broadcasted_iota(jnp.int32, (HALF, 1), 0)
    lane_s = lax.broadcasted_iota(jnp.int32, (HALF, CHUNK), 1)

    def chunk(c, prefetch):
        rows_c = slice(c * CHUNK, (c + 1) * CHUNK)
        hq = h_ref[rows_c, 0:D_MODEL]
        q = hq * _sigmoid(hq)
        f = lb + (1.0 - lb) * _sigmoid(h_ref[rows_c, D_MODEL:2 * D_MODEL])
        kk = 1.0 - f
        iv = h_ref[rows_c, 2 * D_MODEL:3 * D_MODEL]
        log2f = jnp.log2(f)
        b2 = jnp.zeros((CHUNK, D_MODEL), F32)
        for _ in range(3):
            term = log2f.astype(BF16)
            b2 = b2 + jnp.dot(tri, term, preferred_element_type=F32)
            log2f = log2f - term.astype(F32)
        b2_last = b2[CHUNK - 1:CHUNK, :]
        q_ref[...] = q
        b2_ref[...] = b2
        c_ref[...] = b2 - jnp.log2(kk)
        qt_ref[...] = (q * jnp.exp2(b2)).astype(BF16)
        kbar_ref[...] = (kk * jnp.exp2(b2_last - b2)).astype(BF16)
        vb_ref[...] = iv.astype(BF16)

        for i in range(1, N_SUB):
            rows = slice(i * SUB, (i + 1) * SUB)
            ref_i = b2[i * SUB - 1:i * SUB, :]
            qhat_ref[rows, :] = (q[rows] * jnp.exp2(b2[rows] - ref_i)).astype(BF16)
            prev = slice(0, i * SUB)
            khat_ref[i - 1, prev, :] = (kk[prev] * jnp.exp2(ref_i - b2[prev])).astype(BF16)
            khat_ref[i - 1, i * SUB:, :] = jnp.zeros((CHUNK - i * SUB, D_MODEL), BF16)

        for hd in range(C_HEADS):
            hl = slice(hd * C_DIM, (hd + 1) * C_DIM)
            st = state_ref[hd * C_DIM:(hd + 1) * C_DIM, :]
            obuf_ref[rows_c, hl] = lax.dot_general(
                qt_ref[:, hl], st.astype(BF16), NT_DIMS, preferred_element_type=F32)
            wd_ref[hd, 0:SUB, :] = jnp.zeros((SUB, CHUNK), F32)
            for i in range(1, N_SUB):
                wd_ref[hd, i * SUB:(i + 1) * SUB, :] = lax.dot_general(
                    qhat_ref[i * SUB:(i + 1) * SUB, hl], khat_ref[i - 1, :, hl],
                    NT_DIMS, preferred_element_type=F32)
            upd = lax.dot_general(vb_ref[:, hl], kbar_ref[:, hl], TN_DIMS, preferred_element_type=F32)
            state_ref[hd * C_DIM:(hd + 1) * C_DIM, :] = st * jnp.exp2(b2_last[:, hl]) + upd

        for piece, group in prefetch:
            project(piece, group)

        for i in range(N_SUB):
            a = i * SUB
            q_half = (q_ref[a:a + HALF, :], q_ref[a + HALF:a + SUB, :])
            b_half = (b2_ref[a:a + HALF, :], b2_ref[a + HALF:a + SUB, :])
            w = [[jnp.zeros((HALF, CHUNK), F32) for _ in range(C_HEADS)] for _ in range(2)]
            for s in range(SUB):
                c_row = jnp.broadcast_to(c_ref[a + s:a + s + 1, :], (HALF, D_MODEL))
                for half in range(2):
                    first_row = s - half * HALF
                    if first_row >= HALF:
                        continue
                    arg = b_half[half] - c_row
                    if first_row > 0:
                        arg = jnp.where(row8 >= first_row, arg, MASKED_EXPONENT)
                    e_ts = q_half[half] * jnp.exp2(arg)
                    for hd in range(C_HEADS):
                        w_ts = jnp.sum(e_ts[:, hd * C_DIM:(hd + 1) * C_DIM], axis=-1, keepdims=True)
                        w[half][hd] = jnp.where(lane_s == a + s, w_ts, w[half][hd])
            for hd in range(C_HEADS):
                wd_ref[hd, a:a + HALF, :] += w[0][hd]
                wd_ref[hd, a + HALF:a + SUB, :] += w[1][hd]

        for hd in range(C_HEADS):
            hl = slice(hd * C_DIM, (hd + 1) * C_DIM)
            intra = jnp.dot(wd_ref[hd].astype(BF16), vb_ref[:, hl], preferred_element_type=F32)
            obuf_ref[rows_c, hl] += intra

    def finish(piece):
        rows = slice(piece * C_PIECE, (piece + 1) * C_PIECE)
        gz = h_ref[rows, 3 * D_MODEL:]
        gate = gz * _sigmoid(gz)
        for hd in range(C_HEADS):
            hl = slice(hd * C_DIM, (hd + 1) * C_DIM)
            oh = obuf_ref[rows, hl]
            oh = oh * lax.rsqrt(jnp.mean(oh * oh, axis=-1, keepdims=True) + LN_EPS) * gn_ref[...]
            obuf_ref[rows, hl] = oh * gate[:, hl]
        y = jnp.dot(obuf_ref[rows, :].astype(BF16), wo_ref[...], preferred_element_type=F32)
        o_ref[rows, :] = _layer_norm(ALPHA * x_ref[rows, :] + y, g_ref[...], b_ref[...])

    chunks_per_piece = C_PIECE // CHUNK
    groups_per_chunk = N_PROJ // chunks_per_piece
    for group in range(N_PROJ):
        project(0, group)
    for piece in range(C_TM // C_PIECE):
        for cc in range(chunks_per_piece):
            prefetch = []
            if piece + 1 < C_TM // C_PIECE:
                prefetch = [(piece + 1, cc * groups_per_chunk + g) for g in range(groups_per_chunk)]
            chunk(piece * chunks_per_piece + cc, prefetch)
        finish(piece)


def _hgrn_layer(x, w_in, w_o, slab, g_norm, gamma, g, b, batch, layer):
    t = x.shape[0]
    nt = t // batch // C_TM
    return pl.pallas_call(
        functools.partial(_hgrn_kernel, layer=layer),
        grid=(batch, nt),
        in_specs=[
            pl.BlockSpec((C_TM, D_MODEL), lambda bb, i: (bb * nt + i, 0)),
            _layer_slab((D_MODEL, C_IN), slab),
            _layer_slab((D_MODEL, D_MODEL), slab),
            _resident((1, C_DIM)),
            _resident((DEPTH, D_MODEL)),
            _resident((1, D_MODEL)),
            _resident((1, D_MODEL)),
        ],
        out_specs=pl.BlockSpec((C_TM, D_MODEL), lambda bb, i: (bb * nt + i, 0)),
        out_shape=jax.ShapeDtypeStruct((t, D_MODEL), F32),
        scratch_shapes=[
            pltpu.VMEM((C_TM, C_IN), F32),
            pltpu.VMEM((C_HEADS * C_DIM, C_DIM), F32),
            pltpu.VMEM((C_TM, D_MODEL), F32),
            pltpu.VMEM((CHUNK, D_MODEL), F32),
            pltpu.VMEM((CHUNK, D_MODEL), F32),
            pltpu.VMEM((CHUNK, D_MODEL), F32),
            pltpu.VMEM((C_HEADS, CHUNK, CHUNK), F32),
            pltpu.VMEM((CHUNK, D_MODEL), BF16),
            pltpu.VMEM((CHUNK, D_MODEL), BF16),
            pltpu.VMEM((CHUNK, D_MODEL), BF16),
            pltpu.VMEM((N_SUB - 1, CHUNK, D_MODEL), BF16),
            pltpu.VMEM((CHUNK, D_MODEL), BF16),
        ],
        compiler_params=pltpu.CompilerParams(
            dimension_semantics=("arbitrary", "arbitrary"), vmem_limit_bytes=VMEM_LIMIT),
        name="hgrn2_layer",
    )(x, w_in, w_o, g_norm, gamma, g, b)


def kernel(x, positions, ln_gain, ln_bias, ffn1_w_in, ffn1_w_out, ffn2_w_in, ffn2_w_out,
           ab_w_in, ab_w_o, diff_lambda, diff_subln, pool_w, pool_scale,
           c_w_in, c_w_o, c_norm, hgrn_gamma):
    batch, seq, d = x.shape
    t = batch * seq
    x = x.reshape(t, d)
    pos = positions.reshape(t, 1)
    inv_freq = ROPE_THETA ** (-jnp.arange(0, QK_DIM, 2, dtype=F32) / QK_DIM)
    freq = jnp.tile(inv_freq, HEAD_W // (QK_DIM // 2)).reshape(1, HEAD_W)

    def row(a):
        return a.reshape(1, -1)

    ffn1_in, ffn1_out = ffn1_w_in.astype(BF16), ffn1_w_out.astype(BF16)
    ffn2_in, ffn2_out = ffn2_w_in.astype(BF16), ffn2_w_out.astype(BF16)
    ab_in, ab_out = ab_w_in.astype(BF16), ab_w_o.astype(BF16)
    ab_vt = jnp.swapaxes(ab_in[:, :, 2 * A_WIDTH:3 * A_WIDTH], 1, 2)
    pool_wb = pool_w.astype(BF16)
    c_in, c_out = c_w_in.astype(BF16), c_w_o.astype(BF16)

    for l in range(DEPTH):
        j = l // 2
        x = _ffn_ln(x, ffn1_in, ffn1_out, l, row(ln_gain[l, 0]), row(ln_bias[l, 0]))
        if l % 2 == 0:
            lambda_init = 0.8 - 0.6 * math.exp(-0.3 * l)
            qku, vt = _ab_in(x, pos, ab_in, ab_vt, j, freq, pool_wb, row(pool_scale[j]), batch)
            x = _attn_out(qku, vt, x, ab_out, j, diff_lambda[j], diff_subln[j].reshape(-1, 1),
                          row(ln_gain[l, 1]), row(ln_bias[l, 1]), batch, lambda_init)
        else:
            x = _hgrn_layer(x, c_in, c_out, j, row(c_norm[j]),
                            hgrn_gamma, row(ln_gain[l, 1]), row(ln_bias[l, 1]), batch, l)
        x = _ffn_ln(x, ffn2_in, ffn2_out, l, row(ln_gain[l, 2]), row(ln_bias[l, 2]))
    return x.reshape(batch, seq, d)
```

```python
import functools
import math

import numpy as np
import jax
import jax.numpy as jnp
from jax import lax
from jax.experimental import pallas as pl
from jax.experimental.pallas import tpu as pltpu

D_MODEL = 1024
DEPTH = 4
QK_DIM = 64
HEAD_W = 2 * QK_DIM
A_WIDTH = 512
A_HEADS = 4
ROPE_THETA = 10000.0
POOL_WINDOWS = (2, 4, 8, 16)
POOL_HALO = 16
B_WIDTH = 512
POOL_GROUP = 128
AB_IN = 3 * A_WIDTH + B_WIDTH
QKU_W = 2 * A_WIDTH + B_WIDTH
C_DIM = 128
C_HEADS = 8
C_IN = 4 * D_MODEL
CHUNK = 64
SUB = 16
D_FF = 2816
ALPHA = (2 * DEPTH) ** 0.25
LN_EPS = 1e-5

VMEM_LIMIT = 56 * 1024 * 1024

F32 = jnp.float32
BF16 = jnp.bfloat16
NT_DIMS = (((1,), (1,)), ((), ()))
TN_DIMS = (((0,), (0,)), ((), ()))


def _resident(shape):
    return pl.BlockSpec(shape, lambda *_: (0,) * len(shape), pipeline_mode=pl.Buffered(1))


def _layer_slab(shape, layer):
    return pl.BlockSpec((None,) + tuple(shape), lambda *_: (layer,) + (0,) * len(shape),
                        pipeline_mode=pl.Buffered(1))


def _layer_norm(z, g, b):
    mu = jnp.mean(z, axis=-1, keepdims=True)
    d = z - mu
    var = jnp.mean(d * d, axis=-1, keepdims=True)
    return d * lax.rsqrt(var + LN_EPS) * g + b


def _sigmoid(x):
    return 1.0 / (1.0 + jnp.exp(-x))


FFN_TM = 1024
FFN_SUB = 256
FFN_WSTEPS = 11
FFN_WIN_COLS = 2 * D_FF // FFN_WSTEPS
FFN_WOUT_ROWS = D_FF // FFN_WSTEPS


def _ffn_kernel(x_ref, win_ref, wout_ref, g_ref, b_ref, o_ref, winb_ref, woutb_ref, y_ref):
    step = pl.program_id(0)

    @pl.when(step < FFN_WSTEPS)
    def _():
        c0 = pl.multiple_of(step * FFN_WIN_COLS, FFN_WIN_COLS)
        winb_ref[:, pl.ds(c0, FFN_WIN_COLS)] = win_ref[...].astype(BF16)
        r0 = pl.multiple_of(step * FFN_WOUT_ROWS, FFN_WOUT_ROWS)
        woutb_ref[pl.ds(r0, FFN_WOUT_ROWS), :] = wout_ref[...].astype(BF16)

    @pl.when(step >= FFN_WSTEPS)
    def _():
        n_sub = FFN_TM // FFN_SUB
        for s in range(n_sub + 1):
            if s < n_sub:
                x = x_ref[s * FFN_SUB:(s + 1) * FFN_SUB, :]
                h = jnp.dot(x.astype(BF16), winb_ref[...], preferred_element_type=F32)
                gate = h[:, :D_FF]
                up = h[:, D_FF:]
                act = (gate * _sigmoid(gate) * up).astype(BF16)
                y = jnp.dot(act, woutb_ref[...], preferred_element_type=F32)
            if s > 0:
                prev = slice((s - 1) * FFN_SUB, s * FFN_SUB)
                o_ref[prev, :] = _layer_norm(ALPHA * x_ref[prev, :] + 0.5 * y_ref[...], g_ref[...], b_ref[...])
            if s < n_sub:
                y_ref[...] = y


def _ffn_ln(x, w_in, w_out, layer, g, b):
    t = x.shape[0]
    last = FFN_WSTEPS - 1

    def row_tile(i):
        return (jnp.maximum(i - FFN_WSTEPS, 0), 0)

    return pl.pallas_call(
        _ffn_kernel,
        grid=(FFN_WSTEPS + t // FFN_TM,),
        in_specs=[
            pl.BlockSpec((FFN_TM, D_MODEL), row_tile),
            pl.BlockSpec((None, D_MODEL, FFN_WIN_COLS), lambda i: (layer, 0, jnp.minimum(i, last))),
            pl.BlockSpec((None, FFN_WOUT_ROWS, D_MODEL), lambda i: (layer, jnp.minimum(i, last), 0)),
            _resident((1, D_MODEL)),
            _resident((1, D_MODEL)),
        ],
        out_specs=pl.BlockSpec((FFN_TM, D_MODEL), row_tile),
        out_shape=jax.ShapeDtypeStruct((t, D_MODEL), F32),
        scratch_shapes=[
            pltpu.VMEM((D_MODEL, 2 * D_FF), BF16),
            pltpu.VMEM((D_FF, D_MODEL), BF16),
            pltpu.VMEM((FFN_SUB, D_MODEL), F32),
        ],
        compiler_params=pltpu.CompilerParams(
            dimension_semantics=("arbitrary",), vmem_limit_bytes=VMEM_LIMIT),
        name="ffn_ln",
    )(x, w_in, w_out, g, b)


AB_TM = 1024
AB_SUB = 512
Q_SCALE = QK_DIM ** -0.5 * math.log2(math.e)


def _ab_in_kernel(x_ref, pos_ref, w_ref, wvt_ref, freq_ref, pw_ref, ps_ref, o_ref, vt_ref, ubuf_ref):
    i = pl.program_id(1)
    lane = lax.broadcasted_iota(jnp.int32, (AB_SUB, HEAD_W), 1)
    first_half = (lane % QK_DIM) < (QK_DIM // 2)

    @pl.when(i == 0)
    def _():
        ubuf_ref[0:POOL_HALO, :] = jnp.zeros((POOL_HALO, B_WIDTH), F32)

    for s in range(AB_TM // AB_SUB):
        rows = slice(s * AB_SUB, (s + 1) * AB_SUB)
        xb = x_ref[rows, :].astype(BF16)
        h = jnp.dot(xb, w_ref[:, 0:2 * A_WIDTH], preferred_element_type=F32)
        u = jnp.dot(xb, w_ref[:, 3 * A_WIDTH:], preferred_element_type=F32)
        vt_ref[:, rows] = lax.dot_general(wvt_ref[...], xb, NT_DIMS, preferred_element_type=F32).astype(BF16)

        ang = pos_ref[rows, :].astype(F32) * freq_ref[...]
        cos = jnp.cos(ang)
        sin = jnp.sin(ang)
        sin = jnp.where(first_half, -sin, sin)

        def rope(t):
            partner = jnp.where(first_half,
                                pltpu.roll(t, HEAD_W - QK_DIM // 2, axis=1),
                                pltpu.roll(t, QK_DIM // 2, axis=1))
            return t * cos + partner * sin

        for hd in range(A_HEADS):
            sl = slice(hd * HEAD_W, (hd + 1) * HEAD_W)
            q = rope(h[:, sl]) * Q_SCALE
            o_ref[rows, sl] = q.astype(BF16)
            ksl = slice(A_WIDTH + hd * HEAD_W, A_WIDTH + (hd + 1) * HEAD_W)
            o_ref[rows, ksl] = rope(h[:, ksl]).astype(BF16)

        ubuf_ref[POOL_HALO:, :] = u
        seq_idx = i * AB_TM + s * AB_SUB + lax.broadcasted_iota(jnp.int32, (AB_SUB, 1), 0)
        count = (seq_idx + 1).astype(F32)
        for g, w in enumerate(POOL_WINDOWS):
            sl = slice(g * POOL_GROUP, (g + 1) * POOL_GROUP)
            acc = ubuf_ref[:, sl]
            step = 1
            while step < w:
                acc = acc + pltpu.roll(acc, step, axis=0)
                step *= 2
            mean = acc[POOL_HALO:, :] / jnp.minimum(count, float(w))
            d = (mean - u[:, sl]).astype(BF16)
            y = jnp.dot(d, pw_ref[g], preferred_element_type=F32) * ps_ref[:, sl]
            o_ref[rows, 2 * A_WIDTH + g * POOL_GROUP:2 * A_WIDTH + (g + 1) * POOL_GROUP] = y.astype(BF16)
        ubuf_ref[0:POOL_HALO, :] = u[AB_SUB - POOL_HALO:, :]


def _ab_in(x, pos, w_in, w_vt, layer, freq, pool_w, pool_scale, batch):
    t = x.shape[0]
    seq = t // batch
    nt = seq // AB_TM
    return pl.pallas_call(
        _ab_in_kernel,
        grid=(batch, nt),
        in_specs=[
            pl.BlockSpec((AB_TM, D_MODEL), lambda b, i: (b * nt + i, 0)),
            pl.BlockSpec((AB_TM, 1), lambda b, i: (b * nt + i, 0)),
            _layer_slab((D_MODEL, AB_IN), layer),
            _layer_slab((A_WIDTH, D_MODEL), layer),
            _resident((1, HEAD_W)),
            _layer_slab((len(POOL_WINDOWS), POOL_GROUP, POOL_GROUP), layer),
            _resident((1, B_WIDTH)),
        ],
        out_specs=[
            pl.BlockSpec((AB_TM, QKU_W), lambda b, i: (b * nt + i, 0)),
            pl.BlockSpec((None, A_WIDTH, AB_TM), lambda b, i: (b, 0, i)),
        ],
        out_shape=[
            jax.ShapeDtypeStruct((t, QKU_W), BF16),
            jax.ShapeDtypeStruct((batch, A_WIDTH, seq), BF16),
        ],
        scratch_shapes=[pltpu.VMEM((POOL_HALO + AB_SUB, B_WIDTH), F32)],
        compiler_params=pltpu.CompilerParams(
            dimension_semantics=("arbitrary", "arbitrary"), vmem_limit_bytes=VMEM_LIMIT),
        name="ab_in",
    )(x, pos, w_in, w_vt, freq, pool_w, pool_scale)


ATT_TQ = 512
ATT_TK = 512
N_MAPS = 2 * A_HEADS


def _attn_kernel(q_ref, k_ref, vt_ref, ob_ref, x_ref, wo_ref, lam_ref, sub_ref, g_ref, b_ref,
                 o_ref, acc_ref, oat_ref, qz_ref, s_ref, m_ref, l_ref, *, lambda_init):
    qi = pl.program_id(1)
    lv = lam_ref[...]
    lam = (jnp.exp(jnp.sum(lv[0:1] * lv[1:2], axis=-1, keepdims=True))
           - jnp.exp(jnp.sum(lv[2:3] * lv[3:4], axis=-1, keepdims=True)) + lambda_init)

    lane = lax.broadcasted_iota(jnp.int32, (ATT_TQ, HEAD_W), 1)
    key_idx = lax.broadcasted_iota(jnp.int32, (ATT_TK, ATT_TQ), 0)
    qry_idx = lax.broadcasted_iota(jnp.int32, (ATT_TK, ATT_TQ), 1)

    heads = [slice(hd * HEAD_W, (hd + 1) * HEAD_W) for hd in range(A_HEADS)]
    for hd, sl in enumerate(heads):
        qh = q_ref[:, sl]
        qz_ref[2 * hd] = jnp.where(lane < QK_DIM, qh, jnp.zeros_like(qh))
        qz_ref[2 * hd + 1] = jnp.where(lane >= QK_DIM, qh, jnp.zeros_like(qh))
    acc_ref[...] = jnp.zeros(acc_ref.shape, F32)
    m_ref[...] = jnp.full(m_ref.shape, -jnp.inf, F32)
    l_ref[...] = jnp.zeros(l_ref.shape, F32)

    last_block = jnp.maximum(qi - 1, 0)

    def scores(buf, j, i):
        r = pl.multiple_of(j * ATT_TK, ATT_TK)
        kb = k_ref[pl.ds(r, ATT_TK), heads[i // 2]]
        s_ref[buf, i] = lax.dot_general(kb, qz_ref[i], NT_DIMS, preferred_element_type=F32)

    def consume(buf, j, i, masked):
        sc = s_ref[buf, i]
        if masked:
            sc = jnp.where(key_idx <= qry_idx, sc, -jnp.inf)
        m_old = m_ref[i:i + 1, :]
        m_new = jnp.maximum(m_old, jnp.max(sc, axis=0, keepdims=True))
        scale = jnp.exp2(m_old - m_new)
        p = jnp.exp2(sc - m_new)
        l_ref[i:i + 1, :] = scale * l_ref[i:i + 1, :] + jnp.sum(p, axis=0, keepdims=True)
        m_ref[i:i + 1, :] = m_new
        r = pl.multiple_of(j * ATT_TK, ATT_TK)
        vtb = vt_ref[heads[i // 2], pl.ds(r, ATT_TK)]
        acc_ref[i] = scale * acc_ref[i] + jnp.dot(vtb, p.astype(BF16), preferred_element_type=F32)

    def visit(buf, j, masked, prefetch):
        for i in range(N_MAPS):
            if prefetch is not None:
                scores(1 - buf, prefetch, i)
            consume(buf, j, i, masked)

    def pair(t, carry):
        visit(1, 2 * t, False, jnp.minimum(2 * t + 1, last_block))
        visit(0, 2 * t + 1, False, jnp.minimum(2 * t + 2, last_block))
        return carry

    scores(0, qi, 0)
    for i in range(N_MAPS):
        if i + 1 < N_MAPS:
            scores(0, qi, i + 1)
        scores(1, 0, i)
        consume(0, qi, i, True)
    lax.fori_loop(0, qi // 2, pair, 0)

    @pl.when(qi % 2 == 1)
    def _():
        visit(1, qi - 1, False, None)

    for hd, sl in enumerate(heads):
        o = (acc_ref[2 * hd] / l_ref[2 * hd:2 * hd + 1, :]
             - lam * (acc_ref[2 * hd + 1] / l_ref[2 * hd + 1:2 * hd + 2, :]))
        o = o * lax.rsqrt(jnp.mean(o * o, axis=0, keepdims=True) + LN_EPS) * sub_ref[...]
        oat_ref[sl, :] = (o * (1.0 - lambda_init)).astype(BF16)

    y = lax.dot_general(oat_ref[...], wo_ref[0:A_WIDTH, :], TN_DIMS, preferred_element_type=F32)
    y = y + jnp.dot(ob_ref[...], wo_ref[A_WIDTH:, :], preferred_element_type=F32)
    o_ref[...] = _layer_norm(ALPHA * x_ref[...] + y, g_ref[...], b_ref[...])


def _attn_out(qku, vt, x, w_o, layer, lam_vecs, subln, g, b, batch, lambda_init):
    t = x.shape[0]
    seq = t // batch
    nq = seq // ATT_TQ
    return pl.pallas_call(
        functools.partial(_attn_kernel, lambda_init=lambda_init),
        grid=(batch, nq),
        in_specs=[
            pl.BlockSpec((ATT_TQ, A_WIDTH), lambda bb, i: (bb * nq + i, 0)),
            pl.BlockSpec((seq, A_WIDTH), lambda bb, i: (bb, 1), pipeline_mode=pl.Buffered(1)),
            pl.BlockSpec((None, A_WIDTH, seq), lambda bb, i: (bb, 0, 0), pipeline_mode=pl.Buffered(1)),
            pl.BlockSpec((ATT_TQ, B_WIDTH), lambda bb, i: (bb * nq + i, 2)),
            pl.BlockSpec((ATT_TQ, D_MODEL), lambda bb, i: (bb * nq + i, 0)),
            _layer_slab((D_MODEL, D_MODEL), layer),
            _resident((4, QK_DIM)),
            _resident((HEAD_W, 1)),
            _resident((1, D_MODEL)),
            _resident((1, D_MODEL)),
        ],
        out_specs=pl.BlockSpec((ATT_TQ, D_MODEL), lambda bb, i: (bb * nq + i, 0)),
        out_shape=jax.ShapeDtypeStruct((t, D_MODEL), F32),
        scratch_shapes=[
            pltpu.VMEM((N_MAPS, HEAD_W, ATT_TQ), F32),
            pltpu.VMEM((A_WIDTH, ATT_TQ), BF16),
            pltpu.VMEM((N_MAPS, ATT_TQ, HEAD_W), BF16),
            pltpu.VMEM((2, N_MAPS, ATT_TK, ATT_TQ), F32),
            pltpu.VMEM((N_MAPS, ATT_TQ), F32),
            pltpu.VMEM((N_MAPS, ATT_TQ), F32),
        ],
        compiler_params=pltpu.CompilerParams(
            dimension_semantics=("arbitrary", "arbitrary"), vmem_limit_bytes=VMEM_LIMIT),
        name="diff_attn_out",
    )(qku, qku, vt, qku, x, w_o, lam_vecs, subln, g, b)


C_TM = 512
C_PIECE = 128
N_PROJ = C_IN // D_MODEL
N_SUB = CHUNK // SUB
HALF = SUB // 2
MASKED_EXPONENT = -1e30


def _hgrn_kernel(x_ref, win_ref, wo_ref, gn_ref, gamma_ref, g_ref, b_ref, o_ref,
                 h_ref, state_ref, obuf_ref, q_ref, b2_ref, c_ref, wd_ref,
                 qt_ref, kbar_ref, qhat_ref, khat_ref, vb_ref, *, layer):
    @pl.when(pl.program_id(1) == 0)
    def _():
        state_ref[...] = jnp.zeros(state_ref.shape, F32)

    gam = gamma_ref[...]
    e = jnp.exp(gam - jnp.max(gam, axis=0, keepdims=True))
    p = e / jnp.sum(e, axis=0, keepdims=True)
    lb = jnp.sum(p[1:layer + 1], axis=0, keepdims=True)

    def project(piece, group):
        rows = slice(piece * C_PIECE, (piece + 1) * C_PIECE)
        cols = slice(group * D_MODEL, (group + 1) * D_MODEL)
        h_ref[rows, cols] = jnp.dot(x_ref[rows, :].astype(BF16), win_ref[:, cols], preferred_element_type=F32)

    tri = (lax.broadcasted_iota(jnp.int32, (CHUNK, CHUNK), 1)
           <= lax.broadcasted_iota(jnp.int32, (CHUNK, CHUNK), 0)).astype(BF16)
    row8 = lax.broadcasted_iota(jnp.int32, (HALF, 1), 0)
    lane_s = lax.broadcasted_iota(jnp.int32, (HALF, CHUNK), 1)

    def chunk(c, prefetch):
        rows_c = slice(c * CHUNK, (c + 1) * CHUNK)
        hq = h_ref[rows_c, 0:D_MODEL]
        q = hq * _sigmoid(hq)
        f = lb + (1.0 - lb) * _sigmoid(h_ref[rows_c, D_MODEL:2 * D_MODEL])
        kk = 1.0 - f
        iv = h_ref[rows_c, 2 * D_MODEL:3 * D_MODEL]
        log2f = jnp.log2(f)
        b2 = jnp.zeros((CHUNK, D_MODEL), F32)
        for _ in range(3):
            term = log2f.astype(BF16)
            b2 = b2 + jnp.dot(tri, term, preferred_element_type=F32)
            log2f = log2f - term.astype(F32)
        b2_last = b2[CHUNK - 1:CHUNK, :]
        q_ref[...] = q
        b2_ref[...] = b2
        c_ref[...] = b2 - jnp.log2(kk)
        qt_ref[...] = (q * jnp.exp2(b2)).astype(BF16)
        kbar_ref[...] = (kk * jnp.exp2(b2_last - b2)).astype(BF16)
        vb_ref[...] = iv.astype(BF16)

        for i in range(1, N_SUB):
            rows = slice(i * SUB, (i + 1) * SUB)
            ref_i = b2[i * SUB - 1:i * SUB, :]
            qhat_ref[rows, :] = (q[rows] * jnp.exp2(b2[rows] - ref_i)).astype(BF16)
            prev = slice(0, i * SUB)
            khat_ref[i - 1, prev, :] = (kk[prev] * jnp.exp2(ref_i - b2[prev])).astype(BF16)
            khat_ref[i - 1, i * SUB:, :] = jnp.zeros((CHUNK - i * SUB, D_MODEL), BF16)

        for hd in range(C_HEADS):
            hl = slice(hd * C_DIM, (hd + 1) * C_DIM)
            st = state_ref[hd * C_DIM:(hd + 1) * C_DIM, :]
            obuf_ref[rows_c, hl] = lax.dot_general(
                qt_ref[:, hl], st.astype(BF16), NT_DIMS, preferred_element_type=F32)
            wd_ref[hd, 0:SUB, :] = jnp.zeros((SUB, CHUNK), F32)
            for i in range(1, N_SUB):
                wd_ref[hd, i * SUB:(i + 1) * SUB, :] = lax.dot_general(
                    qhat_ref[i * SUB:(i + 1) * SUB, hl], khat_ref[i - 1, :, hl],
                    NT_DIMS, preferred_element_type=F32)
            upd = lax.dot_general(vb_ref[:, hl], kbar_ref[:, hl], TN_DIMS, preferred_element_type=F32)
            state_ref[hd * C_DIM:(hd + 1) * C_DIM, :] = st * jnp.exp2(b2_last[:, hl]) + upd

        for piece, group in prefetch:
            project(piece, group)

        for i in range(N_SUB):
            a = i * SUB
            q_half = (q_ref[a:a + HALF, :], q_ref[a + HALF:a + SUB, :])
            b_half = (b2_ref[a:a + HALF, :], b2_ref[a + HALF:a + SUB, :])
            w = [[jnp.zeros((HALF, CHUNK), F32) for _ in range(C_HEADS)] for _ in range(2)]
            for s in range(SUB):
                c_row = jnp.broadcast_to(c_ref[a + s:a + s + 1, :], (HALF, D_MODEL))
                for half in range(2):
                    first_row = s - half * HALF
                    if first_row >= HALF:
                        continue
                    arg = b_half[half] - c_row
                    if first_row > 0:
                        arg = jnp.where(row8 >= first_row, arg, MASKED_EXPONENT)
                    e_ts = q_half[half] * jnp.exp2(arg)
                    for hd in range(C_HEADS):
                        w_ts = jnp.sum(e_ts[:, hd * C_DIM:(hd + 1) * C_DIM], axis=-1, keepdims=True)
                        w[half][hd] = jnp.where(lane_s == a + s, w_ts, w[half][hd])
            for hd in range(C_HEADS):
                wd_ref[hd, a:a + HALF, :] += w[0][hd]
                wd_ref[hd, a + HALF:a + SUB, :] += w[1][hd]

        for hd in range(C_HEADS):
            hl = slice(hd * C_DIM, (hd + 1) * C_DIM)
            intra = jnp.dot(wd_ref[hd].astype(BF16), vb_ref[:, hl], preferred_element_type=F32)
            obuf_ref[rows_c, hl] += intra

    def finish(piece):
        rows = slice(piece * C_PIECE, (piece + 1) * C_PIECE)
        gz = h_ref[rows, 3 * D_MODEL:]
        gate = gz * _sigmoid(gz)
        for hd in range(C_HEADS):
            hl = slice(hd * C_DIM, (hd + 1) * C_DIM)
            oh = obuf_ref[rows, hl]
            oh = oh * lax.rsqrt(jnp.mean(oh * oh, axis=-1, keepdims=True) + LN_EPS) * gn_ref[...]
            obuf_ref[rows, hl] = oh * gate[:, hl]
        y = jnp.dot(obuf_ref[rows, :].astype(BF16), wo_ref[...], preferred_element_type=F32)
        o_ref[rows, :] = _layer_norm(ALPHA * x_ref[rows, :] + y, g_ref[...], b_ref[...])

    chunks_per_piece = C_PIECE // CHUNK
    groups_per_chunk = N_PROJ // chunks_per_piece
    for group in range(N_PROJ):
        project(0, group)
    for piece in range(C_TM // C_PIECE):
        for cc in range(chunks_per_piece):
            prefetch = []
            if piece + 1 < C_TM // C_PIECE:
                prefetch = [(piece + 1, cc * groups_per_chunk + g) for g in range(groups_per_chunk)]
            chunk(piece * chunks_per_piece + cc, prefetch)
        finish(piece)


def _hgrn_layer(x, w_in, w_o, slab, g_norm, gamma, g, b, batch, layer):
    t = x.shape[0]
    nt = t // batch // C_TM
    return pl.pallas_call(
        functools.partial(_hgrn_kernel, layer=layer),
        grid=(batch, nt),
        in_specs=[
            pl.BlockSpec((C_TM, D_MODEL), lambda bb, i: (bb * nt + i, 0)),
            _layer_slab((D_MODEL, C_IN), slab),
            _layer_slab((D_MODEL, D_MODEL), slab),
            _resident((1, C_DIM)),
            _resident((DEPTH, D_MODEL)),
            _resident((1, D_MODEL)),
            _resident((1, D_MODEL)),
        ],
        out_specs=pl.BlockSpec((C_TM, D_MODEL), lambda bb, i: (bb * nt + i, 0)),
        out_shape=jax.ShapeDtypeStruct((t, D_MODEL), F32),
        scratch_shapes=[
            pltpu.VMEM((C_TM, C_IN), F32),
            pltpu.VMEM((C_HEADS * C_DIM, C_DIM), F32),
            pltpu.VMEM((C_TM, D_MODEL), F32),
            pltpu.VMEM((CHUNK, D_MODEL), F32),
            pltpu.VMEM((CHUNK, D_MODEL), F32),
            pltpu.VMEM((CHUNK, D_MODEL), F32),
            pltpu.VMEM((C_HEADS, CHUNK, CHUNK), F32),
            pltpu.VMEM((CHUNK, D_MODEL), BF16),
            pltpu.VMEM((CHUNK, D_MODEL), BF16),
            pltpu.VMEM((CHUNK, D_MODEL), BF16),
            pltpu.VMEM((N_SUB - 1, CHUNK, D_MODEL), BF16),
            pltpu.VMEM((CHUNK, D_MODEL), BF16),
        ],
        compiler_params=pltpu.CompilerParams(
            dimension_semantics=("arbitrary", "arbitrary"), vmem_limit_bytes=VMEM_LIMIT),
        name="hgrn2_layer",
    )(x, w_in, w_o, g_norm, gamma, g, b)


def kernel(x, positions, ln_gain, ln_bias, ffn1_w_in, ffn1_w_out, ffn2_w_in, ffn2_w_out,
           ab_w_in, ab_w_o, diff_lambda, diff_subln, pool_w, pool_scale,
           c_w_in, c_w_o, c_norm, hgrn_gamma):
    batch, seq, d = x.shape
    t = batch * seq
    x = x.reshape(t, d)
    pos = positions.reshape(t, 1)
    inv_freq = ROPE_THETA ** (-jnp.arange(0, QK_DIM, 2, dtype=F32) / QK_DIM)
    freq = jnp.tile(inv_freq, HEAD_W // (QK_DIM // 2)).reshape(1, HEAD_W)

    def row(a):
        return a.reshape(1, -1)

    ab_in, ab_out = ab_w_in.astype(BF16), ab_w_o.astype(BF16)
    ab_vt = jnp.swapaxes(ab_in[:, :, 2 * A_WIDTH:3 * A_WIDTH], 1, 2)
    pool_wb = pool_w.astype(BF16)
    c_in, c_out = c_w_in.astype(BF16), c_w_o.astype(BF16)

    for l in range(DEPTH):
        j = l // 2
        x = _ffn_ln(x, ffn1_w_in, ffn1_w_out, l, row(ln_gain[l, 0]), row(ln_bias[l, 0]))
        if l % 2 == 0:
            lambda_init = 0.8 - 0.6 * math.exp(-0.3 * l)
            qku, vt = _ab_in(x, pos, ab_in, ab_vt, j, freq, pool_wb, row(pool_scale[j]), batch)
            x = _attn_out(qku, vt, x, ab_out, j, diff_lambda[j], diff_subln[j].reshape(-1, 1),
                          row(ln_gain[l, 1]), row(ln_bias[l, 1]), batch, lambda_init)
        else:
            x = _hgrn_layer(x, c_in, c_out, j, row(c_norm[j]),
                            hgrn_gamma, row(ln_gain[l, 1]), row(ln_bias[l, 1]), batch, l)
        x = _ffn_ln(x, ffn2_w_in, ffn2_w_out, l, row(ln_gain[l, 2]), row(ln_bias[l, 2]))
    return x.reshape(batch, seq, d)
```

```python
import functools
import math

import jax
import jax.numpy as jnp
from jax import lax
from jax.experimental import pallas as pl
from jax.experimental.pallas import tpu as pltpu

D_MODEL = 1024
DEPTH = 4
QK_DIM = 64
HEAD_W = 2 * QK_DIM
A_WIDTH = 512
A_HEADS = 4
ROPE_THETA = 10000.0
POOL_WINDOWS = (2, 4, 8, 16)
POOL_HALO = 16
B_WIDTH = 512
POOL_GROUP = 128
AB_IN = 3 * A_WIDTH + B_WIDTH
QKU_W = 2 * A_WIDTH + B_WIDTH
C_DIM = 128
C_HEADS = 8
C_IN = 4 * D_MODEL
CHUNK = 64
SUB = 16
D_FF = 2816
ALPHA = (2 * DEPTH) ** 0.25
LN_EPS = 1e-5

VMEM_LIMIT = 56 * 1024 * 1024

F32 = jnp.float32
BF16 = jnp.bfloat16
NT_DIMS = (((1,), (1,)), ((), ()))
TN_DIMS = (((0,), (0,)), ((), ()))


def _resident(shape):
    return pl.BlockSpec(shape, lambda *_: (0,) * len(shape), pipeline_mode=pl.Buffered(1))


def _layer_slab(shape, layer):
    return pl.BlockSpec((None,) + tuple(shape), lambda *_: (layer,) + (0,) * len(shape),
                        pipeline_mode=pl.Buffered(1))


def _compiler_params(semantics):
    return pltpu.CompilerParams(dimension_semantics=semantics, vmem_limit_bytes=VMEM_LIMIT)


def _layer_norm(z, g, b):
    mu = jnp.mean(z, axis=-1, keepdims=True)
    d = z - mu
    var = jnp.mean(d * d, axis=-1, keepdims=True)
    return d * lax.rsqrt(var + LN_EPS) * g + b


def _sigmoid(x):
    return 0.5 * jnp.tanh(0.5 * x) + 0.5


FFN_TM = 1024
FFN_SUB = 256
FFN_WSTEPS = 11
FFN_WIN_COLS = 2 * D_FF // FFN_WSTEPS
FFN_WOUT_ROWS = D_FF // FFN_WSTEPS


def _ffn_kernel(x_ref, win_ref, wout_ref, g_ref, b_ref, o_ref, winb_ref, woutb_ref, y_ref):
    step = pl.program_id(0)

    @pl.when(step < FFN_WSTEPS)
    def _():
        c0 = pl.multiple_of(step * FFN_WIN_COLS, FFN_WIN_COLS)
        winb_ref[:, pl.ds(c0, FFN_WIN_COLS)] = win_ref[...].astype(BF16)
        r0 = pl.multiple_of(step * FFN_WOUT_ROWS, FFN_WOUT_ROWS)
        woutb_ref[pl.ds(r0, FFN_WOUT_ROWS), :] = wout_ref[...].astype(BF16)

    @pl.when(step >= FFN_WSTEPS)
    def _():
        n_sub = FFN_TM // FFN_SUB
        for s in range(n_sub + 1):
            if s < n_sub:
                x = x_ref[s * FFN_SUB:(s + 1) * FFN_SUB, :]
                h = jnp.dot(x.astype(BF16), winb_ref[...], preferred_element_type=F32)
                gate = h[:, :D_FF]
                up = h[:, D_FF:]
                act = (gate * _sigmoid(gate) * up).astype(BF16)
                y = jnp.dot(act, woutb_ref[...], preferred_element_type=F32)
            if s > 0:
                prev = slice((s - 1) * FFN_SUB, s * FFN_SUB)
                o_ref[prev, :] = _layer_norm(ALPHA * x_ref[prev, :] + 0.5 * y_ref[...], g_ref[...], b_ref[...])
            if s < n_sub:
                y_ref[...] = y


def _ffn_ln(x, w_in, w_out, layer, g, b):
    t = x.shape[0]
    last = FFN_WSTEPS - 1

    def row_tile(i):
        return (jnp.maximum(i - FFN_WSTEPS, 0), 0)

    return pl.pallas_call(
        _ffn_kernel,
        grid=(FFN_WSTEPS + t // FFN_TM,),
        in_specs=[
            pl.BlockSpec((FFN_TM, D_MODEL), row_tile),
            pl.BlockSpec((None, D_MODEL, FFN_WIN_COLS), lambda i: (layer, 0, jnp.minimum(i, last))),
            pl.BlockSpec((None, FFN_WOUT_ROWS, D_MODEL), lambda i: (layer, jnp.minimum(i, last), 0)),
            _resident((1, D_MODEL)),
            _resident((1, D_MODEL)),
        ],
        out_specs=pl.BlockSpec((FFN_TM, D_MODEL), row_tile),
        out_shape=jax.ShapeDtypeStruct((t, D_MODEL), F32),
        scratch_shapes=[
            pltpu.VMEM((D_MODEL, 2 * D_FF), BF16),
            pltpu.VMEM((D_FF, D_MODEL), BF16),
            pltpu.VMEM((FFN_SUB, D_MODEL), F32),
        ],
        compiler_params=_compiler_params(("arbitrary",)),
        name="ffn_ln",
    )(x, w_in, w_out, g, b)


AB_TM = 1024
AB_SUB = 512
Q_SCALE = QK_DIM ** -0.5 * math.log2(math.e)


def _ab_in_kernel(x_ref, pos_ref, w_ref, wvt_ref, freq_ref, pw_ref, ps_ref, o_ref, vt_ref, ubuf_ref):
    i = pl.program_id(1)
    lane = lax.broadcasted_iota(jnp.int32, (AB_SUB, HEAD_W), 1)
    first_half = (lane % QK_DIM) < (QK_DIM // 2)

    @pl.when(i == 0)
    def _():
        ubuf_ref[0:POOL_HALO, :] = jnp.zeros((POOL_HALO, B_WIDTH), F32)

    for s in range(AB_TM // AB_SUB):
        rows = slice(s * AB_SUB, (s + 1) * AB_SUB)
        xb = x_ref[rows, :].astype(BF16)
        h = jnp.dot(xb, w_ref[:, 0:2 * A_WIDTH], preferred_element_type=F32)
        u = jnp.dot(xb, w_ref[:, 3 * A_WIDTH:], preferred_element_type=F32)
        vt_ref[:, rows] = lax.dot_general(wvt_ref[...], xb, NT_DIMS, preferred_element_type=F32).astype(BF16)

        ang = pos_ref[rows, :].astype(F32) * freq_ref[...]
        cos = jnp.cos(ang)
        sin = jnp.sin(ang)
        sin = jnp.where(first_half, -sin, sin)

        def rope(t):
            partner = jnp.where(first_half,
                                pltpu.roll(t, HEAD_W - QK_DIM // 2, axis=1),
                                pltpu.roll(t, QK_DIM // 2, axis=1))
            return t * cos + partner * sin

        for hd in range(A_HEADS):
            sl = slice(hd * HEAD_W, (hd + 1) * HEAD_W)
            q = rope(h[:, sl]) * Q_SCALE
            o_ref[rows, sl] = q.astype(BF16)
            ksl = slice(A_WIDTH + hd * HEAD_W, A_WIDTH + (hd + 1) * HEAD_W)
            o_ref[rows, ksl] = rope(h[:, ksl]).astype(BF16)

        ubuf_ref[POOL_HALO:, :] = u
        seq_idx = i * AB_TM + s * AB_SUB + lax.broadcasted_iota(jnp.int32, (AB_SUB, 1), 0)
        count = (seq_idx + 1).astype(F32)
        for g, w in enumerate(POOL_WINDOWS):
            sl = slice(g * POOL_GROUP, (g + 1) * POOL_GROUP)
            acc = ubuf_ref[:, sl]
            step = 1
            while step < w:
                acc = acc + pltpu.roll(acc, step, axis=0)
                step *= 2
            mean = acc[POOL_HALO:, :] / jnp.minimum(count, float(w))
            d = (mean - u[:, sl]).astype(BF16)
            y = jnp.dot(d, pw_ref[g], preferred_element_type=F32) * ps_ref[:, sl]
            o_ref[rows, 2 * A_WIDTH + g * POOL_GROUP:2 * A_WIDTH + (g + 1) * POOL_GROUP] = y.astype(BF16)
        ubuf_ref[0:POOL_HALO, :] = u[AB_SUB - POOL_HALO:, :]


def _ab_in(x, pos, w_in, w_vt, layer, freq, pool_w, pool_scale, batch):
    t = x.shape[0]
    seq = t // batch
    nt = seq // AB_TM
    return pl.pallas_call(
        _ab_in_kernel,
        grid=(batch, nt),
        in_specs=[
            pl.BlockSpec((AB_TM, D_MODEL), lambda b, i: (b * nt + i, 0)),
            pl.BlockSpec((AB_TM, 1), lambda b, i: (b * nt + i, 0)),
            _layer_slab((D_MODEL, AB_IN), layer),
            _layer_slab((A_WIDTH, D_MODEL), layer),
            _resident((1, HEAD_W)),
            _layer_slab((len(POOL_WINDOWS), POOL_GROUP, POOL_GROUP), layer),
            _resident((1, B_WIDTH)),
        ],
        out_specs=[
            pl.BlockSpec((AB_TM, QKU_W), lambda b, i: (b * nt + i, 0)),
            pl.BlockSpec((None, A_WIDTH, AB_TM), lambda b, i: (b, 0, i)),
        ],
        out_shape=[
            jax.ShapeDtypeStruct((t, QKU_W), BF16),
            jax.ShapeDtypeStruct((batch, A_WIDTH, seq), BF16),
        ],
        scratch_shapes=[pltpu.VMEM((POOL_HALO + AB_SUB, B_WIDTH), F32)],
        compiler_params=_compiler_params(("arbitrary", "arbitrary")),
        name="ab_in",
    )(x, pos, w_in, w_vt, freq, pool_w, pool_scale)


ATT_TQ = 512
ATT_TK = 512
N_MAPS = 2 * A_HEADS


def _attn_kernel(q_ref, k_ref, vt_ref, ob_ref, x_ref, wo_ref, lam_ref, sub_ref, g_ref, b_ref,
                 o_ref, acc_ref, oat_ref, qz_ref, s_ref, m_ref, l_ref, *, lambda_init):
    qi = pl.program_id(1)
    lv = lam_ref[...]
    lam = (jnp.exp(jnp.sum(lv[0:1] * lv[1:2], axis=-1, keepdims=True))
           - jnp.exp(jnp.sum(lv[2:3] * lv[3:4], axis=-1, keepdims=True)) + lambda_init)

    lane = lax.broadcasted_iota(jnp.int32, (ATT_TQ, HEAD_W), 1)
    key_idx = lax.broadcasted_iota(jnp.int32, (ATT_TK, ATT_TQ), 0)
    qry_idx = lax.broadcasted_iota(jnp.int32, (ATT_TK, ATT_TQ), 1)

    heads = [slice(hd * HEAD_W, (hd + 1) * HEAD_W) for hd in range(A_HEADS)]
    for hd, sl in enumerate(heads):
        qh = q_ref[:, sl]
        qz_ref[2 * hd] = jnp.where(lane < QK_DIM, qh, jnp.zeros_like(qh))
        qz_ref[2 * hd + 1] = jnp.where(lane >= QK_DIM, qh, jnp.zeros_like(qh))
    acc_ref[...] = jnp.zeros(acc_ref.shape, F32)
    m_ref[...] = jnp.full(m_ref.shape, -jnp.inf, F32)
    l_ref[...] = jnp.zeros(l_ref.shape, F32)

    last_block = jnp.maximum(qi - 1, 0)

    def scores(buf, j, i):
        r = pl.multiple_of(j * ATT_TK, ATT_TK)
        kb = k_ref[pl.ds(r, ATT_TK), heads[i // 2]]
        s_ref[buf, i] = lax.dot_general(kb, qz_ref[i], NT_DIMS, preferred_element_type=F32)

    def consume(buf, j, i, masked):
        sc = s_ref[buf, i]
        if masked:
            sc = jnp.where(key_idx <= qry_idx, sc, -jnp.inf)
        m_old = m_ref[i:i + 1, :]
        m_new = jnp.maximum(m_old, jnp.max(sc, axis=0, keepdims=True))
        scale = jnp.exp2(m_old - m_new)
        p = jnp.exp2(sc - m_new)
        l_ref[i:i + 1, :] = scale * l_ref[i:i + 1, :] + jnp.sum(p, axis=0, keepdims=True)
        m_ref[i:i + 1, :] = m_new
        r = pl.multiple_of(j * ATT_TK, ATT_TK)
        vtb = vt_ref[heads[i // 2], pl.ds(r, ATT_TK)]
        acc_ref[i] = scale * acc_ref[i] + jnp.dot(vtb, p.astype(BF16), preferred_element_type=F32)

    def visit(buf, j, masked, prefetch):
        for i in range(N_MAPS):
            if prefetch is not None:
                scores(1 - buf, prefetch, i)
            consume(buf, j, i, masked)

    def pair(t, carry):
        visit(1, 2 * t, False, jnp.minimum(2 * t + 1, last_block))
        visit(0, 2 * t + 1, False, jnp.minimum(2 * t + 2, last_block))
        return carry

    scores(0, qi, 0)
    for i in range(N_MAPS):
        if i + 1 < N_MAPS:
            scores(0, qi, i + 1)
        scores(1, 0, i)
        consume(0, qi, i, True)
    lax.fori_loop(0, qi // 2, pair, 0)

    @pl.when(qi % 2 == 1)
    def _():
        visit(1, qi - 1, False, None)

    for hd, sl in enumerate(heads):
        o = (acc_ref[2 * hd] / l_ref[2 * hd:2 * hd + 1, :]
             - lam * (acc_ref[2 * hd + 1] / l_ref[2 * hd + 1:2 * hd + 2, :]))
        o = o * lax.rsqrt(jnp.mean(o * o, axis=0, keepdims=True) + LN_EPS) * sub_ref[...]
        oat_ref[sl, :] = (o * (1.0 - lambda_init)).astype(BF16)

    y = lax.dot_general(oat_ref[...], wo_ref[0:A_WIDTH, :], TN_DIMS, preferred_element_type=F32)
    y = y + jnp.dot(ob_ref[...], wo_ref[A_WIDTH:, :], preferred_element_type=F32)
    o_ref[...] = _layer_norm(ALPHA * x_ref[...] + y, g_ref[...], b_ref[...])


def _attn_out(qku, vt, x, w_o, layer, lam_vecs, subln, g, b, batch, lambda_init):
    t = x.shape[0]
    seq = t // batch
    nq = seq // ATT_TQ
    return pl.pallas_call(
        functools.partial(_attn_kernel, lambda_init=lambda_init),
        grid=(batch, nq),
        in_specs=[
            pl.BlockSpec((ATT_TQ, A_WIDTH), lambda bb, i: (bb * nq + i, 0)),
            pl.BlockSpec((seq, A_WIDTH), lambda bb, i: (bb, 1), pipeline_mode=pl.Buffered(1)),
            pl.BlockSpec((None, A_WIDTH, seq), lambda bb, i: (bb, 0, 0), pipeline_mode=pl.Buffered(1)),
            pl.BlockSpec((ATT_TQ, B_WIDTH), lambda bb, i: (bb * nq + i, 2)),
            pl.BlockSpec((ATT_TQ, D_MODEL), lambda bb, i: (bb * nq + i, 0)),
            _layer_slab((D_MODEL, D_MODEL), layer),
            _resident((4, QK_DIM)),
            _resident((HEAD_W, 1)),
            _resident((1, D_MODEL)),
            _resident((1, D_MODEL)),
        ],
        out_specs=pl.BlockSpec((ATT_TQ, D_MODEL), lambda bb, i: (bb * nq + i, 0)),
        out_shape=jax.ShapeDtypeStruct((t, D_MODEL), F32),
        scratch_shapes=[
            pltpu.VMEM((N_MAPS, HEAD_W, ATT_TQ), F32),
            pltpu.VMEM((A_WIDTH, ATT_TQ), BF16),
            pltpu.VMEM((N_MAPS, ATT_TQ, HEAD_W), BF16),
            pltpu.VMEM((2, N_MAPS, ATT_TK, ATT_TQ), F32),
            pltpu.VMEM((N_MAPS, ATT_TQ), F32),
            pltpu.VMEM((N_MAPS, ATT_TQ), F32),
        ],
        compiler_params=_compiler_params(("arbitrary", "arbitrary")),
        name="diff_attn_out",
    )(qku, qku, vt, qku, x, w_o, lam_vecs, subln, g, b)


C_TM = 512
C_PIECE = 128
N_PROJ = C_IN // D_MODEL
LOG_DECAY_TERMS = 2
N_SUB = CHUNK // SUB
HALF = SUB // 2
MASKED_EXPONENT = -1e30


def _hgrn_kernel(x_ref, xn_ref, win_ref, wo_ref, gn_ref, gamma_ref, g_ref, b_ref, o_ref,
                 h0_ref, h_ref, state_ref, obuf_ref, q_ref, b2_ref, c_ref, wd_ref,
                 qt_ref, kbar_ref, qhat_ref, khat_ref, vb_ref, *, layer):
    @pl.when(pl.program_id(1) == 0)
    def _():
        state_ref[...] = jnp.zeros(state_ref.shape, F32)

    gam = gamma_ref[...]
    e = jnp.exp(gam - jnp.max(gam, axis=0, keepdims=True))
    p = e / jnp.sum(e, axis=0, keepdims=True)
    lb = jnp.sum(p[1:layer + 1], axis=0, keepdims=True)

    def h_rows(piece, lo, hi):
        if piece == 0:
            return h0_ref, slice(lo, hi)
        base = (piece - 1) * C_PIECE
        return h_ref, slice(base + lo, base + hi)

    def project(piece, group, next_tile=False):
        cols = slice(group * D_MODEL, (group + 1) * D_MODEL)
        src = xn_ref[...] if next_tile else x_ref[piece * C_PIECE:(piece + 1) * C_PIECE, :]
        dst_ref, rows = h_rows(piece, 0, C_PIECE)
        dst_ref[rows, cols] = jnp.dot(src.astype(BF16), win_ref[:, cols], preferred_element_type=F32)

    chunks_per_piece = C_PIECE // CHUNK
    tri = (lax.broadcasted_iota(jnp.int32, (CHUNK, CHUNK), 1)
           <= lax.broadcasted_iota(jnp.int32, (CHUNK, CHUNK), 0)).astype(BF16)
    row8 = lax.broadcasted_iota(jnp.int32, (HALF, 1), 0)
    lane_s = lax.broadcasted_iota(jnp.int32, (HALF, CHUNK), 1)

    def chunk(c, prefetch):
        rows_c = slice(c * CHUNK, (c + 1) * CHUNK)
        lo = (c % chunks_per_piece) * CHUNK
        hc_ref, hrows = h_rows(c // chunks_per_piece, lo, lo + CHUNK)
        hq = hc_ref[hrows, 0:D_MODEL]
        q = hq * _sigmoid(hq)
        f = lb + (1.0 - lb) * _sigmoid(hc_ref[hrows, D_MODEL:2 * D_MODEL])
        kk = 1.0 - f
        iv = hc_ref[hrows, 2 * D_MODEL:3 * D_MODEL]
        log2f = jnp.log2(f)
        b2 = jnp.zeros((CHUNK, D_MODEL), F32)
        for _ in range(LOG_DECAY_TERMS):
            term = log2f.astype(BF16)
            b2 = b2 + jnp.dot(tri, term, preferred_element_type=F32)
            log2f = log2f - term.astype(F32)
        b2_last = b2[CHUNK - 1:CHUNK, :]
        q_ref[...] = q
        b2_ref[...] = b2
        c_ref[...] = b2 - jnp.log2(kk)
        qt_ref[...] = (q * jnp.exp2(b2)).astype(BF16)
        kbar_ref[...] = (kk * jnp.exp2(b2_last - b2)).astype(BF16)
        vb_ref[...] = iv.astype(BF16)

        for i in range(1, N_SUB):
            rows = slice(i * SUB, (i + 1) * SUB)
            ref_i = b2[i * SUB - 1:i * SUB, :]
            qhat_ref[rows, :] = (q[rows] * jnp.exp2(b2[rows] - ref_i)).astype(BF16)
            prev = slice(0, i * SUB)
            khat_ref[i - 1, prev, :] = (kk[prev] * jnp.exp2(ref_i - b2[prev])).astype(BF16)
            khat_ref[i - 1, i * SUB:, :] = jnp.zeros((CHUNK - i * SUB, D_MODEL), BF16)

        for hd in range(C_HEADS):
            hl = slice(hd * C_DIM, (hd + 1) * C_DIM)
            st = state_ref[hd * C_DIM:(hd + 1) * C_DIM, :]
            obuf_ref[rows_c, hl] = lax.dot_general(
                qt_ref[:, hl], st.astype(BF16), NT_DIMS, preferred_element_type=F32)
            wd_ref[hd, 0:SUB, :] = jnp.zeros((SUB, CHUNK), F32)
            for i in range(1, N_SUB):
                wd_ref[hd, i * SUB:(i + 1) * SUB, :] = lax.dot_general(
                    qhat_ref[i * SUB:(i + 1) * SUB, hl], khat_ref[i - 1, :, hl],
                    NT_DIMS, preferred_element_type=F32)
            upd = lax.dot_general(vb_ref[:, hl], kbar_ref[:, hl], TN_DIMS, preferred_element_type=F32)
            state_ref[hd * C_DIM:(hd + 1) * C_DIM, :] = st * jnp.exp2(b2_last[:, hl]) + upd

        for piece, group, next_tile in prefetch:
            project(piece, group, next_tile)

        for i in range(N_SUB):
            a = i * SUB
            q_half = (q_ref[a:a + HALF, :], q_ref[a + HALF:a + SUB, :])
            b_half = (b2_ref[a:a + HALF, :], b2_ref[a + HALF:a + SUB, :])
            w = [[jnp.zeros((HALF, CHUNK), F32) for _ in range(C_HEADS)] for _ in range(2)]
            for s in range(SUB):
                c_row = jnp.broadcast_to(c_ref[a + s:a + s + 1, :], (HALF, D_MODEL))
                for half in range(2):
                    first_row = s - half * HALF
                    if first_row >= HALF:
                        continue
                    arg = b_half[half] - c_row
                    if first_row > 0:
                        arg = jnp.where(row8 >= first_row, arg, MASKED_EXPONENT)
                    e_ts = q_half[half] * jnp.exp2(arg)
                    for hd in range(C_HEADS):
                        w_ts = jnp.sum(e_ts[:, hd * C_DIM:(hd + 1) * C_DIM], axis=-1, keepdims=True)
                        w[half][hd] = jnp.where(lane_s == a + s, w_ts, w[half][hd])
            for hd in range(C_HEADS):
                wd_ref[hd, a:a + HALF, :] += w[0][hd]
                wd_ref[hd, a + HALF:a + SUB, :] += w[1][hd]

        for hd in range(C_HEADS):
            hl = slice(hd * C_DIM, (hd + 1) * C_DIM)
            intra = jnp.dot(wd_ref[hd].astype(BF16), vb_ref[:, hl], preferred_element_type=F32)
            obuf_ref[rows_c, hl] += intra

    def finish(piece):
        rows = slice(piece * C_PIECE, (piece + 1) * C_PIECE)
        hp_ref, hrows = h_rows(piece, 0, C_PIECE)
        gz = hp_ref[hrows, 3 * D_MODEL:]
        gate = gz * _sigmoid(gz)
        for hd in range(C_HEADS):
            hl = slice(hd * C_DIM, (hd + 1) * C_DIM)
            oh = obuf_ref[rows, hl]
            oh = oh * lax.rsqrt(jnp.mean(oh * oh, axis=-1, keepdims=True) + LN_EPS) * gn_ref[...]
            obuf_ref[rows, hl] = oh * gate[:, hl]
        y = jnp.dot(obuf_ref[rows, :].astype(BF16), wo_ref[...], preferred_element_type=F32)
        o_ref[rows, :] = _layer_norm(ALPHA * x_ref[rows, :] + y, g_ref[...], b_ref[...])

    @pl.when((pl.program_id(0) == 0) & (pl.program_id(1) == 0))
    def _():
        for group in range(N_PROJ):
            project(0, group)

    n_pieces = C_TM // C_PIECE
    groups_per_chunk = N_PROJ // chunks_per_piece
    for piece in range(n_pieces):
        for cc in range(chunks_per_piece):
            groups = [cc * groups_per_chunk + g for g in range(groups_per_chunk)]
            if piece + 1 < n_pieces:
                prefetch = [(piece + 1, g, False) for g in groups]
            else:
                prefetch = [(0, g, True) for g in groups]
            chunk(piece * chunks_per_piece + cc, prefetch)
        finish(piece)


def _hgrn_layer(x, w_in, w_o, slab, g_norm, gamma, g, b, batch, layer):
    t = x.shape[0]
    nt = t // batch // C_TM
    pieces_per_tile = C_TM // C_PIECE

    def next_first_piece(bb, i):
        return (jnp.minimum((bb * nt + i + 1) * pieces_per_tile, t // C_PIECE - 1), 0)

    return pl.pallas_call(
        functools.partial(_hgrn_kernel, layer=layer),
        grid=(batch, nt),
        in_specs=[
            pl.BlockSpec((C_TM, D_MODEL), lambda bb, i: (bb * nt + i, 0)),
            pl.BlockSpec((C_PIECE, D_MODEL), next_first_piece),
            _layer_slab((D_MODEL, C_IN), slab),
            _layer_slab((D_MODEL, D_MODEL), slab),
            _resident((1, C_DIM)),
            _resident((DEPTH, D_MODEL)),
            _resident((1, D_MODEL)),
            _resident((1, D_MODEL)),
        ],
        out_specs=pl.BlockSpec((C_TM, D_MODEL), lambda bb, i: (bb * nt + i, 0)),
        out_shape=jax.ShapeDtypeStruct((t, D_MODEL), F32),
        scratch_shapes=[
            pltpu.VMEM((C_PIECE, C_IN), F32),
            pltpu.VMEM((C_TM - C_PIECE, C_IN), F32),
            pltpu.VMEM((C_HEADS * C_DIM, C_DIM), F32),
            pltpu.VMEM((C_TM, D_MODEL), F32),
            pltpu.VMEM((CHUNK, D_MODEL), F32),
            pltpu.VMEM((CHUNK, D_MODEL), F32),
            pltpu.VMEM((CHUNK, D_MODEL), F32),
            pltpu.VMEM((C_HEADS, CHUNK, CHUNK), F32),
            pltpu.VMEM((CHUNK, D_MODEL), BF16),
            pltpu.VMEM((CHUNK, D_MODEL), BF16),
            pltpu.VMEM((CHUNK, D_MODEL), BF16),
            pltpu.VMEM((N_SUB - 1, CHUNK, D_MODEL), BF16),
            pltpu.VMEM((CHUNK, D_MODEL), BF16),
        ],
        compiler_params=_compiler_params(("arbitrary", "arbitrary")),
        name="hgrn2_layer",
    )(x, x, w_in, w_o, g_norm, gamma, g, b)


def kernel(x, positions, ln_gain, ln_bias, ffn1_w_in, ffn1_w_out, ffn2_w_in, ffn2_w_out,
           ab_w_in, ab_w_o, diff_lambda, diff_subln, pool_w, pool_scale,
           c_w_in, c_w_o, c_norm, hgrn_gamma):
    batch, seq, d = x.shape
    t = batch * seq
    x = x.reshape(t, d)
    pos = positions.reshape(t, 1)
    inv_freq = ROPE_THETA ** (-jnp.arange(0, QK_DIM, 2, dtype=F32) / QK_DIM)
    freq = jnp.tile(inv_freq, HEAD_W // (QK_DIM // 2)).reshape(1, HEAD_W)

    def row(a):
        return a.reshape(1, -1)

    ab_in, ab_out = ab_w_in.astype(BF16), ab_w_o.astype(BF16)
    ab_vt = jnp.swapaxes(ab_w_in[:, :, 2 * A_WIDTH:3 * A_WIDTH], 1, 2).astype(BF16)
    pool_wb = pool_w.astype(BF16)
    c_in, c_out = c_w_in.astype(BF16), c_w_o.astype(BF16)

    for l in range(DEPTH):
        j = l // 2
        x = _ffn_ln(x, ffn1_w_in, ffn1_w_out, l, row(ln_gain[l, 0]), row(ln_bias[l, 0]))
        if l % 2 == 0:
            lambda_init = 0.8 - 0.6 * math.exp(-0.3 * l)
            qku, vt = _ab_in(x, pos, ab_in, ab_vt, j, freq, pool_wb, row(pool_scale[j]), batch)
            x = _attn_out(qku, vt, x, ab_out, j, diff_lambda[j], diff_subln[j].reshape(-1, 1),
                          row(ln_gain[l, 1]), row(ln_bias[l, 1]), batch, lambda_init)
        else:
            x = _hgrn_layer(x, c_in, c_out, j, row(c_norm[j]),
                            hgrn_gamma, row(ln_gain[l, 1]), row(ln_bias[l, 1]), batch, l)
        x = _ffn_ln(x, ffn2_w_in, ffn2_w_out, l, row(ln_gain[l, 2]), row(ln_bias[l, 2]))
    return x.reshape(batch, seq, d)
```

```python
import functools
import math

import jax
import jax.numpy as jnp
from jax import lax
from jax.experimental import pallas as pl
from jax.experimental.pallas import tpu as pltpu

D_MODEL = 1024
DEPTH = 4
QK_DIM = 64
HEAD_W = 2 * QK_DIM
A_WIDTH = 512
A_HEADS = 4
ROPE_THETA = 10000.0
POOL_WINDOWS = (2, 4, 8, 16)
POOL_HALO = 16
B_WIDTH = 512
POOL_GROUP = 128
AB_IN = 3 * A_WIDTH + B_WIDTH
QKU_W = 2 * A_WIDTH + B_WIDTH
C_DIM = 128
C_HEADS = 8
C_IN = 4 * D_MODEL
CHUNK = 64
SUB = 16
D_FF = 2816
ALPHA = (2 * DEPTH) ** 0.25
LN_EPS = 1e-5

VMEM_LIMIT = 56 * 1024 * 1024

F32 = jnp.float32
BF16 = jnp.bfloat16
NT_DIMS = (((1,), (1,)), ((), ()))
TN_DIMS = (((0,), (0,)), ((), ()))


def _resident(shape):
    return pl.BlockSpec(shape, lambda *_: (0,) * len(shape), pipeline_mode=pl.Buffered(1))


def _layer_slab(shape, layer):
    return pl.BlockSpec((None,) + tuple(shape), lambda *_: (layer,) + (0,) * len(shape),
                        pipeline_mode=pl.Buffered(1))


def _compiler_params(semantics):
    return pltpu.CompilerParams(dimension_semantics=semantics, vmem_limit_bytes=VMEM_LIMIT)


def _layer_norm(z, g, b):
    mu = jnp.mean(z, axis=-1, keepdims=True)
    d = z - mu
    var = jnp.mean(d * d, axis=-1, keepdims=True)
    return d * lax.rsqrt(var + LN_EPS) * g + b


def _sigmoid(x):
    return 0.5 * jnp.tanh(0.5 * x) + 0.5


FFN_TM = 1024
FFN_SUB = 256
FFN_WSTEPS = 11
FFN_WIN_COLS = 2 * D_FF // FFN_WSTEPS
FFN_WOUT_ROWS = D_FF // FFN_WSTEPS


def _ffn_kernel(x_ref, win_ref, wout_ref, g_ref, b_ref, o_ref, winb_ref, woutb_ref, y_ref, *, sub):
    step = pl.program_id(0)

    @pl.when(step < FFN_WSTEPS)
    def _():
        c0 = pl.multiple_of(step * FFN_WIN_COLS, FFN_WIN_COLS)
        winb_ref[:, pl.ds(c0, FFN_WIN_COLS)] = win_ref[...].astype(BF16)
        r0 = pl.multiple_of(step * FFN_WOUT_ROWS, FFN_WOUT_ROWS)
        woutb_ref[pl.ds(r0, FFN_WOUT_ROWS), :] = wout_ref[...].astype(BF16)

    @pl.when(step >= FFN_WSTEPS)
    def _():
        n_sub = FFN_TM // sub
        for s in range(n_sub + 1):
            if s < n_sub:
                x = x_ref[s * sub:(s + 1) * sub, :]
                h = jnp.dot(x.astype(BF16), winb_ref[...], preferred_element_type=F32)
                gate = h[:, :D_FF]
                up = h[:, D_FF:]
                act = (gate * _sigmoid(gate) * up).astype(BF16)
                y = jnp.dot(act, woutb_ref[...], preferred_element_type=F32)
            if s > 0:
                prev = slice((s - 1) * sub, s * sub)
                o_ref[prev, :] = _layer_norm(ALPHA * x_ref[prev, :] + 0.5 * y_ref[...], g_ref[...], b_ref[...])
            if s < n_sub:
                y_ref[...] = y


def _ffn_ln(x, w_in, w_out, layer, g, b, sub=FFN_SUB):
    t = x.shape[0]
    last = FFN_WSTEPS - 1

    def row_tile(i):
        return (jnp.maximum(i - FFN_WSTEPS, 0), 0)

    return pl.pallas_call(
        functools.partial(_ffn_kernel, sub=sub),
        grid=(FFN_WSTEPS + t // FFN_TM,),
        in_specs=[
            pl.BlockSpec((FFN_TM, D_MODEL), row_tile),
            pl.BlockSpec((None, D_MODEL, FFN_WIN_COLS), lambda i: (layer, 0, jnp.minimum(i, last))),
            pl.BlockSpec((None, FFN_WOUT_ROWS, D_MODEL), lambda i: (layer, jnp.minimum(i, last), 0)),
            _resident((1, D_MODEL)),
            _resident((1, D_MODEL)),
        ],
        out_specs=pl.BlockSpec((FFN_TM, D_MODEL), row_tile),
        out_shape=jax.ShapeDtypeStruct((t, D_MODEL), F32),
        scratch_shapes=[
            pltpu.VMEM((D_MODEL, 2 * D_FF), BF16),
            pltpu.VMEM((D_FF, D_MODEL), BF16),
            pltpu.VMEM((sub, D_MODEL), F32),
        ],
        compiler_params=_compiler_params(("arbitrary",)),
        name="ffn_ln",
    )(x, w_in, w_out, g, b)


AB_TM = 1024
AB_SUB = 512
Q_SCALE = QK_DIM ** -0.5 * math.log2(math.e)


def _ab_in_kernel(x_ref, pos_ref, w_ref, wvt_ref, freq_ref, pw_ref, ps_ref, o_ref, vt_ref, ubuf_ref):
    i = pl.program_id(1)
    lane = lax.broadcasted_iota(jnp.int32, (AB_SUB, HEAD_W), 1)
    first_half = (lane % QK_DIM) < (QK_DIM // 2)

    @pl.when(i == 0)
    def _():
        ubuf_ref[0:POOL_HALO, :] = jnp.zeros((POOL_HALO, B_WIDTH), F32)

    for s in range(AB_TM // AB_SUB):
        rows = slice(s * AB_SUB, (s + 1) * AB_SUB)
        xb = x_ref[rows, :].astype(BF16)
        h = jnp.dot(xb, w_ref[:, 0:2 * A_WIDTH].astype(BF16), preferred_element_type=F32)
        u = jnp.dot(xb, w_ref[:, 3 * A_WIDTH:].astype(BF16), preferred_element_type=F32)
        vt_ref[:, rows] = lax.dot_general(wvt_ref[...], xb, NT_DIMS, preferred_element_type=F32).astype(BF16)

        ang = pos_ref[rows, :].astype(F32) * freq_ref[...]
        cos = jnp.cos(ang)
        sin = jnp.sin(ang)
        sin = jnp.where(first_half, -sin, sin)

        def rope(t):
            partner = jnp.where(first_half,
                                pltpu.roll(t, HEAD_W - QK_DIM // 2, axis=1),
                                pltpu.roll(t, QK_DIM // 2, axis=1))
            return t * cos + partner * sin

        for hd in range(A_HEADS):
            sl = slice(hd * HEAD_W, (hd + 1) * HEAD_W)
            q = rope(h[:, sl]) * Q_SCALE
            o_ref[rows, sl] = q.astype(BF16)
            ksl = slice(A_WIDTH + hd * HEAD_W, A_WIDTH + (hd + 1) * HEAD_W)
            o_ref[rows, ksl] = rope(h[:, ksl]).astype(BF16)

        ubuf_ref[POOL_HALO:, :] = u
        seq_idx = i * AB_TM + s * AB_SUB + lax.broadcasted_iota(jnp.int32, (AB_SUB, 1), 0)
        count = (seq_idx + 1).astype(F32)
        for g, w in enumerate(POOL_WINDOWS):
            sl = slice(g * POOL_GROUP, (g + 1) * POOL_GROUP)
            acc = ubuf_ref[:, sl]
            step = 1
            while step < w:
                acc = acc + pltpu.roll(acc, step, axis=0)
                step *= 2
            mean = acc[POOL_HALO:, :] / jnp.minimum(count, float(w))
            d = (mean - u[:, sl]).astype(BF16)
            y = jnp.dot(d, pw_ref[g], preferred_element_type=F32) * ps_ref[:, sl]
            o_ref[rows, 2 * A_WIDTH + g * POOL_GROUP:2 * A_WIDTH + (g + 1) * POOL_GROUP] = y.astype(BF16)
        ubuf_ref[0:POOL_HALO, :] = u[AB_SUB - POOL_HALO:, :]


def _ab_in(x, pos, w_in, w_vt, layer, freq, pool_w, pool_scale, batch):
    t = x.shape[0]
    seq = t // batch
    nt = seq // AB_TM
    return pl.pallas_call(
        _ab_in_kernel,
        grid=(batch, nt),
        in_specs=[
            pl.BlockSpec((AB_TM, D_MODEL), lambda b, i: (b * nt + i, 0)),
            pl.BlockSpec((AB_TM, 1), lambda b, i: (b * nt + i, 0)),
            _layer_slab((D_MODEL, AB_IN), layer),
            _layer_slab((A_WIDTH, D_MODEL), layer),
            _resident((1, HEAD_W)),
            _layer_slab((len(POOL_WINDOWS), POOL_GROUP, POOL_GROUP), layer),
            _resident((1, B_WIDTH)),
        ],
        out_specs=[
            pl.BlockSpec((AB_TM, QKU_W), lambda b, i: (b * nt + i, 0)),
            pl.BlockSpec((None, A_WIDTH, AB_TM), lambda b, i: (b, 0, i)),
        ],
        out_shape=[
            jax.ShapeDtypeStruct((t, QKU_W), BF16),
            jax.ShapeDtypeStruct((batch, A_WIDTH, seq), BF16),
        ],
        scratch_shapes=[pltpu.VMEM((POOL_HALO + AB_SUB, B_WIDTH), F32)],
        compiler_params=_compiler_params(("arbitrary", "arbitrary")),
        name="ab_in",
    )(x, pos, w_in, w_vt, freq, pool_w, pool_scale)


ATT_TQ = 512
ATT_TK = 512
N_MAPS = 2 * A_HEADS


def _attn_kernel(q_ref, k_ref, vt_ref, ob_ref, x_ref, wo_ref, lam_ref, sub_ref, g_ref, b_ref,
                 o_ref, acc_ref, oat_ref, qz_ref, s_ref, m_ref, l_ref, *, lambda_init, q_transposed):
    qi = pl.program_id(1)
    lv = lam_ref[...]
    lam = (jnp.exp(jnp.sum(lv[0:1] * lv[1:2], axis=-1, keepdims=True))
           - jnp.exp(jnp.sum(lv[2:3] * lv[3:4], axis=-1, keepdims=True)) + lambda_init)

    lane = lax.broadcasted_iota(jnp.int32, (ATT_TQ, HEAD_W), 1)
    key_idx = lax.broadcasted_iota(jnp.int32, (ATT_TK, ATT_TQ), 0)
    qry_idx = lax.broadcasted_iota(jnp.int32, (ATT_TK, ATT_TQ), 1)

    heads = [slice(hd * HEAD_W, (hd + 1) * HEAD_W) for hd in range(A_HEADS)]
    for hd, sl in enumerate(heads):
        qh = q_ref[:, sl]
        for c, keep in enumerate((lane < QK_DIM, lane >= QK_DIM)):
            qz = jnp.where(keep, qh, jnp.zeros_like(qh))
            qz_ref[2 * hd + c] = qz.T if q_transposed else qz
    acc_ref[...] = jnp.zeros(acc_ref.shape, F32)
    m_ref[...] = jnp.full(m_ref.shape, -jnp.inf, F32)
    l_ref[...] = jnp.zeros(l_ref.shape, F32)

    last_block = jnp.maximum(qi - 1, 0)

    def scores(buf, j, i):
        r = pl.multiple_of(j * ATT_TK, ATT_TK)
        kb = k_ref[pl.ds(r, ATT_TK), heads[i // 2]]
        if q_transposed:
            s_ref[buf, i] = jnp.dot(kb, qz_ref[i], preferred_element_type=F32)
        else:
            s_ref[buf, i] = lax.dot_general(kb, qz_ref[i], NT_DIMS, preferred_element_type=F32)

    def consume(buf, j, i, masked):
        sc = s_ref[buf, i]
        if masked:
            sc = jnp.where(key_idx <= qry_idx, sc, -jnp.inf)
        m_old = m_ref[i:i + 1, :]
        m_new = jnp.maximum(m_old, jnp.max(sc, axis=0, keepdims=True))
        scale = jnp.exp2(m_old - m_new)
        p = jnp.exp2(sc - m_new)
        l_ref[i:i + 1, :] = scale * l_ref[i:i + 1, :] + jnp.sum(p, axis=0, keepdims=True)
        m_ref[i:i + 1, :] = m_new
        r = pl.multiple_of(j * ATT_TK, ATT_TK)
        vtb = vt_ref[heads[i // 2], pl.ds(r, ATT_TK)]
        acc_ref[i] = scale * acc_ref[i] + jnp.dot(vtb, p.astype(BF16), preferred_element_type=F32)

    def visit(buf, j, masked, prefetch):
        for i in range(N_MAPS):
            if prefetch is not None:
                scores(1 - buf, prefetch, i)
            consume(buf, j, i, masked)

    def pair(t, carry):
        visit(1, 2 * t, False, jnp.minimum(2 * t + 1, last_block))
        visit(0, 2 * t + 1, False, jnp.minimum(2 * t + 2, last_block))
        return carry

    scores(0, qi, 0)
    for i in range(N_MAPS):
        if i + 1 < N_MAPS:
            scores(0, qi, i + 1)
        scores(1, 0, i)
        consume(0, qi, i, True)
    lax.fori_loop(0, qi // 2, pair, 0)

    @pl.when(qi % 2 == 1)
    def _():
        visit(1, qi - 1, False, None)

    for hd, sl in enumerate(heads):
        o = (acc_ref[2 * hd] / l_ref[2 * hd:2 * hd + 1, :]
             - lam * (acc_ref[2 * hd + 1] / l_ref[2 * hd + 1:2 * hd + 2, :]))
        o = o * lax.rsqrt(jnp.mean(o * o, axis=0, keepdims=True) + LN_EPS) * sub_ref[...]
        oat_ref[sl, :] = (o * (1.0 - lambda_init)).astype(BF16)

    y = lax.dot_general(oat_ref[...], wo_ref[0:A_WIDTH, :], TN_DIMS, preferred_element_type=F32)
    y = y + jnp.dot(ob_ref[...], wo_ref[A_WIDTH:, :], preferred_element_type=F32)
    o_ref[...] = _layer_norm(ALPHA * x_ref[...] + y, g_ref[...], b_ref[...])


def _attn_out(qku, vt, x, w_o, layer, lam_vecs, subln, g, b, batch, lambda_init, q_transposed=False):
    t = x.shape[0]
    seq = t // batch
    nq = seq // ATT_TQ
    return pl.pallas_call(
        functools.partial(_attn_kernel, lambda_init=lambda_init, q_transposed=q_transposed),
        grid=(batch, nq),
        in_specs=[
            pl.BlockSpec((ATT_TQ, A_WIDTH), lambda bb, i: (bb * nq + i, 0)),
            pl.BlockSpec((seq, A_WIDTH), lambda bb, i: (bb, 1), pipeline_mode=pl.Buffered(1)),
            pl.BlockSpec((None, A_WIDTH, seq), lambda bb, i: (bb, 0, 0), pipeline_mode=pl.Buffered(1)),
            pl.BlockSpec((ATT_TQ, B_WIDTH), lambda bb, i: (bb * nq + i, 2)),
            pl.BlockSpec((ATT_TQ, D_MODEL), lambda bb, i: (bb * nq + i, 0)),
            _layer_slab((D_MODEL, D_MODEL), layer),
            _resident((4, QK_DIM)),
            _resident((HEAD_W, 1)),
            _resident((1, D_MODEL)),
            _resident((1, D_MODEL)),
        ],
        out_specs=pl.BlockSpec((ATT_TQ, D_MODEL), lambda bb, i: (bb * nq + i, 0)),
        out_shape=jax.ShapeDtypeStruct((t, D_MODEL), F32),
        scratch_shapes=[
            pltpu.VMEM((N_MAPS, HEAD_W, ATT_TQ), F32),
            pltpu.VMEM((A_WIDTH, ATT_TQ), BF16),
            pltpu.VMEM((N_MAPS, HEAD_W, ATT_TQ) if q_transposed else (N_MAPS, ATT_TQ, HEAD_W), BF16),
            pltpu.VMEM((2, N_MAPS, ATT_TK, ATT_TQ), F32),
            pltpu.VMEM((N_MAPS, ATT_TQ), F32),
            pltpu.VMEM((N_MAPS, ATT_TQ), F32),
        ],
        compiler_params=_compiler_params(("arbitrary", "arbitrary")),
        name="diff_attn_out",
    )(qku, qku, vt, qku, x, w_o, lam_vecs, subln, g, b)


C_TM = 512
C_PIECE = 128
N_PROJ = C_IN // D_MODEL
LOG_DECAY_TERMS = 2
N_SUB = CHUNK // SUB
HALF = SUB // 2
MASKED_EXPONENT = -1e30


def _hgrn_kernel(x_ref, xn_ref, win_ref, wo_ref, gn_ref, gamma_ref, g_ref, b_ref, o_ref,
                 h0_ref, h_ref, state_ref, obuf_ref, q_ref, b2_ref, c_ref, wd_ref,
                 qt_ref, kbar_ref, qhat_ref, khat_ref, vb_ref, *, layer, piece_rows):
    @pl.when(pl.program_id(1) == 0)
    def _():
        state_ref[...] = jnp.zeros(state_ref.shape, F32)

    gam = gamma_ref[...]
    e = jnp.exp(gam - jnp.max(gam, axis=0, keepdims=True))
    p = e / jnp.sum(e, axis=0, keepdims=True)
    lb = jnp.sum(p[1:layer + 1], axis=0, keepdims=True)

    def h_rows(piece, lo, hi):
        if piece == 0:
            return h0_ref, slice(lo, hi)
        base = (piece - 1) * piece_rows
        return h_ref, slice(base + lo, base + hi)

    def project(piece, group, next_tile=False):
        cols = slice(group * D_MODEL, (group + 1) * D_MODEL)
        src = xn_ref[...] if next_tile else x_ref[piece * piece_rows:(piece + 1) * piece_rows, :]
        dst_ref, rows = h_rows(piece, 0, piece_rows)
        dst_ref[rows, cols] = jnp.dot(src.astype(BF16), win_ref[:, cols], preferred_element_type=F32)

    chunks_per_piece = piece_rows // CHUNK
    tri = (lax.broadcasted_iota(jnp.int32, (CHUNK, CHUNK), 1)
           <= lax.broadcasted_iota(jnp.int32, (CHUNK, CHUNK), 0)).astype(BF16)
    row8 = lax.broadcasted_iota(jnp.int32, (HALF, 1), 0)
    lane_s = lax.broadcasted_iota(jnp.int32, (HALF, CHUNK), 1)

    def chunk(c, prefetch):
        rows_c = slice(c * CHUNK, (c + 1) * CHUNK)
        lo = (c % chunks_per_piece) * CHUNK
        hc_ref, hrows = h_rows(c // chunks_per_piece, lo, lo + CHUNK)
        hq = hc_ref[hrows, 0:D_MODEL]
        q = hq * _sigmoid(hq)
        f = lb + (1.0 - lb) * _sigmoid(hc_ref[hrows, D_MODEL:2 * D_MODEL])
        kk = 1.0 - f
        iv = hc_ref[hrows, 2 * D_MODEL:3 * D_MODEL]
        log2f = jnp.log2(f)
        b2 = jnp.zeros((CHUNK, D_MODEL), F32)
        for _ in range(LOG_DECAY_TERMS):
            term = log2f.astype(BF16)
            b2 = b2 + jnp.dot(tri, term, preferred_element_type=F32)
            log2f = log2f - term.astype(F32)
        b2_last = b2[CHUNK - 1:CHUNK, :]
        q_ref[...] = q
        b2_ref[...] = b2
        c_ref[...] = b2 - jnp.log2(kk)
        qt_ref[...] = (q * jnp.exp2(b2)).astype(BF16)
        kbar_ref[...] = (kk * jnp.exp2(b2_last - b2)).astype(BF16)
        vb_ref[...] = iv.astype(BF16)

        for i in range(1, N_SUB):
            rows = slice(i * SUB, (i + 1) * SUB)
            ref_i = b2[i * SUB - 1:i * SUB, :]
            qhat_ref[rows, :] = (q[rows] * jnp.exp2(b2[rows] - ref_i)).astype(BF16)
            prev = slice(0, i * SUB)
            khat_ref[i - 1, prev, :] = (kk[prev] * jnp.exp2(ref_i - b2[prev])).astype(BF16)
            khat_ref[i - 1, i * SUB:, :] = jnp.zeros((CHUNK - i * SUB, D_MODEL), BF16)

        for hd in range(C_HEADS):
            hl = slice(hd * C_DIM, (hd + 1) * C_DIM)
            st = state_ref[hd * C_DIM:(hd + 1) * C_DIM, :]
            obuf_ref[rows_c, hl] = lax.dot_general(
                qt_ref[:, hl], st.astype(BF16), NT_DIMS, preferred_element_type=F32)
            wd_ref[hd, 0:SUB, :] = jnp.zeros((SUB, CHUNK), F32)
            for i in range(1, N_SUB):
                wd_ref[hd, i * SUB:(i + 1) * SUB, :] = lax.dot_general(
                    qhat_ref[i * SUB:(i + 1) * SUB, hl], khat_ref[i - 1, :, hl],
                    NT_DIMS, preferred_element_type=F32)
            upd = lax.dot_general(vb_ref[:, hl], kbar_ref[:, hl], TN_DIMS, preferred_element_type=F32)
            state_ref[hd * C_DIM:(hd + 1) * C_DIM, :] = st * jnp.exp2(b2_last[:, hl]) + upd

        for piece, group, next_tile in prefetch:
            project(piece, group, next_tile)

        for i in range(N_SUB):
            a = i * SUB
            q_half = (q_ref[a:a + HALF, :], q_ref[a + HALF:a + SUB, :])
            b_half = (b2_ref[a:a + HALF, :], b2_ref[a + HALF:a + SUB, :])
            w = [[jnp.zeros((HALF, CHUNK), F32) for _ in range(C_HEADS)] for _ in range(2)]
            for s in range(SUB):
                c_row = jnp.broadcast_to(c_ref[a + s:a + s + 1, :], (HALF, D_MODEL))
                for half in range(2):
                    first_row = s - half * HALF
                    if first_row >= HALF:
                        continue
                    arg = b_half[half] - c_row
                    if first_row > 0:
                        arg = jnp.where(row8 >= first_row, arg, MASKED_EXPONENT)
                    e_ts = q_half[half] * jnp.exp2(arg)
                    for hd in range(C_HEADS):
                        w_ts = jnp.sum(e_ts[:, hd * C_DIM:(hd + 1) * C_DIM], axis=-1, keepdims=True)
                        w[half][hd] = jnp.where(lane_s == a + s, w_ts, w[half][hd])
            for hd in range(C_HEADS):
                wd_ref[hd, a:a + HALF, :] += w[0][hd]
                wd_ref[hd, a + HALF:a + SUB, :] += w[1][hd]

        for hd in range(C_HEADS):
            hl = slice(hd * C_DIM, (hd + 1) * C_DIM)
            intra = jnp.dot(wd_ref[hd].astype(BF16), vb_ref[:, hl], preferred_element_type=F32)
            obuf_ref[rows_c, hl] += intra

    def finish(piece):
        rows = slice(piece * piece_rows, (piece + 1) * piece_rows)
        hp_ref, hrows = h_rows(piece, 0, piece_rows)
        gz = hp_ref[hrows, 3 * D_MODEL:]
        gate = gz * _sigmoid(gz)
        for hd in range(C_HEADS):
            hl = slice(hd * C_DIM, (hd + 1) * C_DIM)
            oh = obuf_ref[rows, hl]
            oh = oh * lax.rsqrt(jnp.mean(oh * oh, axis=-1, keepdims=True) + LN_EPS) * gn_ref[...]
            obuf_ref[rows, hl] = oh * gate[:, hl]
        y = jnp.dot(obuf_ref[rows, :].astype(BF16), wo_ref[...], preferred_element_type=F32)
        o_ref[rows, :] = _layer_norm(ALPHA * x_ref[rows, :] + y, g_ref[...], b_ref[...])

    @pl.when((pl.program_id(0) == 0) & (pl.program_id(1) == 0))
    def _():
        for group in range(N_PROJ):
            project(0, group)

    n_pieces = C_TM // piece_rows
    groups_per_chunk = N_PROJ // chunks_per_piece
    for piece in range(n_pieces):
        for cc in range(chunks_per_piece):
            groups = [cc * groups_per_chunk + g for g in range(groups_per_chunk)]
            if piece + 1 < n_pieces:
                prefetch = [(piece + 1, g, False) for g in groups]
            else:
                prefetch = [(0, g, True) for g in groups]
            chunk(piece * chunks_per_piece + cc, prefetch)
        finish(piece)


def _hgrn_layer(x, w_in, w_o, slab, g_norm, gamma, g, b, batch, layer, piece_rows=C_PIECE):
    t = x.shape[0]
    nt = t // batch // C_TM
    pieces_per_tile = C_TM // piece_rows

    def next_first_piece(bb, i):
        return (jnp.minimum((bb * nt + i + 1) * pieces_per_tile, t // piece_rows - 1), 0)

    return pl.pallas_call(
        functools.partial(_hgrn_kernel, layer=layer, piece_rows=piece_rows),
        grid=(batch, nt),
        in_specs=[
            pl.BlockSpec((C_TM, D_MODEL), lambda bb, i: (bb * nt + i, 0)),
            pl.BlockSpec((piece_rows, D_MODEL), next_first_piece),
            _layer_slab((D_MODEL, C_IN), slab),
            _layer_slab((D_MODEL, D_MODEL), slab),
            _resident((1, C_DIM)),
            _resident((DEPTH, D_MODEL)),
            _resident((1, D_MODEL)),
            _resident((1, D_MODEL)),
        ],
        out_specs=pl.BlockSpec((C_TM, D_MODEL), lambda bb, i: (bb * nt + i, 0)),
        out_shape=jax.ShapeDtypeStruct((t, D_MODEL), F32),
        scratch_shapes=[
            pltpu.VMEM((piece_rows, C_IN), F32),
            pltpu.VMEM((C_TM - piece_rows, C_IN), F32),
            pltpu.VMEM((C_HEADS * C_DIM, C_DIM), F32),
            pltpu.VMEM((C_TM, D_MODEL), F32),
            pltpu.VMEM((CHUNK, D_MODEL), F32),
            pltpu.VMEM((CHUNK, D_MODEL), F32),
            pltpu.VMEM((CHUNK, D_MODEL), F32),
            pltpu.VMEM((C_HEADS, CHUNK, CHUNK), F32),
            pltpu.VMEM((CHUNK, D_MODEL), BF16),
            pltpu.VMEM((CHUNK, D_MODEL), BF16),
            pltpu.VMEM((CHUNK, D_MODEL), BF16),
            pltpu.VMEM((N_SUB - 1, CHUNK, D_MODEL), BF16),
            pltpu.VMEM((CHUNK, D_MODEL), BF16),
        ],
        compiler_params=_compiler_params(("arbitrary", "arbitrary")),
        name="hgrn2_layer",
    )(x, x, w_in, w_o, g_norm, gamma, g, b)


def kernel(x, positions, ln_gain, ln_bias, ffn1_w_in, ffn1_w_out, ffn2_w_in, ffn2_w_out,
           ab_w_in, ab_w_o, diff_lambda, diff_subln, pool_w, pool_scale,
           c_w_in, c_w_o, c_norm, hgrn_gamma):
    batch, seq, d = x.shape
    t = batch * seq
    x = x.reshape(t, d)
    pos = positions.reshape(t, 1)
    inv_freq = ROPE_THETA ** (-jnp.arange(0, QK_DIM, 2, dtype=F32) / QK_DIM)
    freq = jnp.tile(inv_freq, HEAD_W // (QK_DIM // 2)).reshape(1, HEAD_W)

    def row(a):
        return a.reshape(1, -1)

    ab_out = ab_w_o.astype(BF16)
    ab_vt = jnp.swapaxes(ab_w_in[:, :, 2 * A_WIDTH:3 * A_WIDTH], 1, 2).astype(BF16)
    pool_wb = pool_w.astype(BF16)
    c_in, c_out = c_w_in.astype(BF16), c_w_o.astype(BF16)

    for l in range(DEPTH):
        j = l // 2
        x = _ffn_ln(x, ffn1_w_in, ffn1_w_out, l, row(ln_gain[l, 0]), row(ln_bias[l, 0]),
                    sub=(512 if l == 0 else FFN_SUB))
        if l % 2 == 0:
            lambda_init = 0.8 - 0.6 * math.exp(-0.3 * l)
            qku, vt = _ab_in(x, pos, ab_w_in, ab_vt, j, freq, pool_wb, row(pool_scale[j]), batch)
            x = _attn_out(qku, vt, x, ab_out, j, diff_lambda[j], diff_subln[j].reshape(-1, 1),
                          row(ln_gain[l, 1]), row(ln_bias[l, 1]), batch, lambda_init, q_transposed=(l == 2))
        else:
            x = _hgrn_layer(x, c_in, c_out, j, row(c_norm[j]),
                            hgrn_gamma, row(ln_gain[l, 1]), row(ln_bias[l, 1]), batch, l,
                            piece_rows=(256 if l == 3 else C_PIECE))
        x = _ffn_ln(x, ffn2_w_in, ffn2_w_out, l, row(ln_gain[l, 2]), row(ln_bias[l, 2]))
    return x.reshape(batch, seq, d)
```

```python
import functools
import math

import jax
import jax.numpy as jnp
from jax import lax
from jax.experimental import pallas as pl
from jax.experimental.pallas import tpu as pltpu

D_MODEL = 1024
DEPTH = 4
QK_DIM = 64
HEAD_W = 2 * QK_DIM
A_WIDTH = 512
A_HEADS = 4
ROPE_THETA = 10000.0
POOL_WINDOWS = (2, 4, 8, 16)
POOL_HALO = 16
B_WIDTH = 512
POOL_GROUP = 128
AB_IN = 3 * A_WIDTH + B_WIDTH
QKU_W = 2 * A_WIDTH + B_WIDTH
C_DIM = 128
C_HEADS = 8
C_IN = 4 * D_MODEL
CHUNK = 64
SUB = 16
D_FF = 2816
ALPHA = (2 * DEPTH) ** 0.25
LN_EPS = 1e-5

VMEM_LIMIT = 56 * 1024 * 1024

F32 = jnp.float32
BF16 = jnp.bfloat16
NT_DIMS = (((1,), (1,)), ((), ()))
TN_DIMS = (((0,), (0,)), ((), ()))


def _resident(shape):
    return pl.BlockSpec(shape, lambda *_: (0,) * len(shape), pipeline_mode=pl.Buffered(1))


def _layer_slab(shape, layer):
    return pl.BlockSpec((None,) + tuple(shape), lambda *_: (layer,) + (0,) * len(shape),
                        pipeline_mode=pl.Buffered(1))


def _compiler_params(semantics):
    return pltpu.CompilerParams(dimension_semantics=semantics, vmem_limit_bytes=VMEM_LIMIT)


def _layer_norm(z, g, b):
    mu = jnp.mean(z, axis=-1, keepdims=True)
    d = z - mu
    var = jnp.mean(d * d, axis=-1, keepdims=True)
    return d * lax.rsqrt(var + LN_EPS) * g + b


def _sigmoid(x):
    return 0.5 * jnp.tanh(0.5 * x) + 0.5


FFN_TM = 1024
FFN_SUB = 256
FFN_WSTEPS = 11
FFN_WIN_COLS = 2 * D_FF // FFN_WSTEPS
FFN_WOUT_ROWS = D_FF // FFN_WSTEPS


def _ffn_kernel(x_ref, win_ref, wout_ref, g_ref, b_ref, o_ref, winb_ref, woutb_ref, y_ref):
    step = pl.program_id(0)

    @pl.when(step < FFN_WSTEPS)
    def _():
        c0 = pl.multiple_of(step * FFN_WIN_COLS, FFN_WIN_COLS)
        winb_ref[:, pl.ds(c0, FFN_WIN_COLS)] = win_ref[...].astype(BF16)
        r0 = pl.multiple_of(step * FFN_WOUT_ROWS, FFN_WOUT_ROWS)
        woutb_ref[pl.ds(r0, FFN_WOUT_ROWS), :] = wout_ref[...].astype(BF16)

    @pl.when(step >= FFN_WSTEPS)
    def _():
        n_sub = FFN_TM // FFN_SUB
        for s in range(n_sub + 1):
            if s < n_sub:
                x = x_ref[s * FFN_SUB:(s + 1) * FFN_SUB, :]
                h = jnp.dot(x.astype(BF16), winb_ref[...], preferred_element_type=F32)
                gate = h[:, :D_FF]
                up = h[:, D_FF:]
                act = (gate * _sigmoid(gate) * up).astype(BF16)
                y = jnp.dot(act, woutb_ref[...], preferred_element_type=F32)
            if s > 0:
                prev = slice((s - 1) * FFN_SUB, s * FFN_SUB)
                o_ref[prev, :] = _layer_norm(ALPHA * x_ref[prev, :] + 0.5 * y_ref[...], g_ref[...], b_ref[...])
            if s < n_sub:
                y_ref[...] = y


def _ffn_ln(x, w_in, w_out, layer, g, b):
    t = x.shape[0]
    last = FFN_WSTEPS - 1

    def row_tile(i):
        return (jnp.maximum(i - FFN_WSTEPS, 0), 0)

    return pl.pallas_call(
        _ffn_kernel,
        grid=(FFN_WSTEPS + t // FFN_TM,),
        in_specs=[
            pl.BlockSpec((FFN_TM, D_MODEL), row_tile),
            pl.BlockSpec((None, D_MODEL, FFN_WIN_COLS), lambda i: (layer, 0, jnp.minimum(i, last))),
            pl.BlockSpec((None, FFN_WOUT_ROWS, D_MODEL), lambda i: (layer, jnp.minimum(i, last), 0)),
            _resident((1, D_MODEL)),
            _resident((1, D_MODEL)),
        ],
        out_specs=pl.BlockSpec((FFN_TM, D_MODEL), row_tile),
        out_shape=jax.ShapeDtypeStruct((t, D_MODEL), F32),
        scratch_shapes=[
            pltpu.VMEM((D_MODEL, 2 * D_FF), BF16),
            pltpu.VMEM((D_FF, D_MODEL), BF16),
            pltpu.VMEM((FFN_SUB, D_MODEL), F32),
        ],
        compiler_params=_compiler_params(("arbitrary",)),
        name="ffn_ln",
    )(x, w_in, w_out, g, b)


AB_TM = 1024
AB_SUB = 512
Q_SCALE = QK_DIM ** -0.5 * math.log2(math.e)


def _ab_in_kernel(x_ref, pos_ref, w_ref, wvt_ref, freq_ref, pw_ref, ps_ref, o_ref, vt_ref, ubuf_ref):
    i = pl.program_id(1)
    lane = lax.broadcasted_iota(jnp.int32, (AB_SUB, HEAD_W), 1)
    first_half = (lane % QK_DIM) < (QK_DIM // 2)

    @pl.when(i == 0)
    def _():
        ubuf_ref[0:POOL_HALO, :] = jnp.zeros((POOL_HALO, B_WIDTH), F32)

    for s in range(AB_TM // AB_SUB):
        rows = slice(s * AB_SUB, (s + 1) * AB_SUB)
        xb = x_ref[rows, :].astype(BF16)
        h = jnp.dot(xb, w_ref[:, 0:2 * A_WIDTH].astype(BF16), preferred_element_type=F32)
        u = jnp.dot(xb, w_ref[:, 3 * A_WIDTH:].astype(BF16), preferred_element_type=F32)
        vt_ref[:, rows] = lax.dot_general(wvt_ref[...], xb, NT_DIMS, preferred_element_type=F32).astype(BF16)

        ang = pos_ref[rows, :].astype(F32) * freq_ref[...]
        cos = jnp.cos(ang)
        sin = jnp.sin(ang)
        sin = jnp.where(first_half, -sin, sin)

        def rope(t):
            partner = jnp.where(first_half,
                                pltpu.roll(t, HEAD_W - QK_DIM // 2, axis=1),
                                pltpu.roll(t, QK_DIM // 2, axis=1))
            return t * cos + partner * sin

        for hd in range(A_HEADS):
            sl = slice(hd * HEAD_W, (hd + 1) * HEAD_W)
            q = rope(h[:, sl]) * Q_SCALE
            o_ref[rows, sl] = q.astype(BF16)
            ksl = slice(A_WIDTH + hd * HEAD_W, A_WIDTH + (hd + 1) * HEAD_W)
            o_ref[rows, ksl] = rope(h[:, ksl]).astype(BF16)

        ubuf_ref[POOL_HALO:, :] = u
        seq_idx = i * AB_TM + s * AB_SUB + lax.broadcasted_iota(jnp.int32, (AB_SUB, 1), 0)
        count = (seq_idx + 1).astype(F32)
        for g, w in enumerate(POOL_WINDOWS):
            sl = slice(g * POOL_GROUP, (g + 1) * POOL_GROUP)
            acc = ubuf_ref[:, sl]
            step = 1
            while step < w:
                acc = acc + pltpu.roll(acc, step, axis=0)
                step *= 2
            mean = acc[POOL_HALO:, :] / jnp.minimum(count, float(w))
            d = (mean - u[:, sl]).astype(BF16)
            y = jnp.dot(d, pw_ref[g], preferred_element_type=F32) * ps_ref[:, sl]
            o_ref[rows, 2 * A_WIDTH + g * POOL_GROUP:2 * A_WIDTH + (g + 1) * POOL_GROUP] = y.astype(BF16)
        ubuf_ref[0:POOL_HALO, :] = u[AB_SUB - POOL_HALO:, :]


def _ab_in(x, pos, w_in, w_vt, layer, freq, pool_w, pool_scale, batch):
    t = x.shape[0]
    seq = t // batch
    nt = seq // AB_TM
    return pl.pallas_call(
        _ab_in_kernel,
        grid=(batch, nt),
        in_specs=[
            pl.BlockSpec((AB_TM, D_MODEL), lambda b, i: (b * nt + i, 0)),
            pl.BlockSpec((AB_TM, 1), lambda b, i: (b * nt + i, 0)),
            _layer_slab((D_MODEL, AB_IN), layer),
            _layer_slab((A_WIDTH, D_MODEL), layer),
            _resident((1, HEAD_W)),
            _layer_slab((len(POOL_WINDOWS), POOL_GROUP, POOL_GROUP), layer),
            _resident((1, B_WIDTH)),
        ],
        out_specs=[
            pl.BlockSpec((AB_TM, QKU_W), lambda b, i: (b * nt + i, 0)),
            pl.BlockSpec((None, A_WIDTH, AB_TM), lambda b, i: (b, 0, i)),
        ],
        out_shape=[
            jax.ShapeDtypeStruct((t, QKU_W), BF16),
            jax.ShapeDtypeStruct((batch, A_WIDTH, seq), BF16),
        ],
        scratch_shapes=[pltpu.VMEM((POOL_HALO + AB_SUB, B_WIDTH), F32)],
        compiler_params=_compiler_params(("arbitrary", "arbitrary")),
        name="ab_in",
    )(x, pos, w_in, w_vt, freq, pool_w, pool_scale)


ATT_TQ = 512
ATT_TK = 512
N_MAPS = 2 * A_HEADS


def _attn_kernel(q_ref, k_ref, vt_ref, ob_ref, x_ref, wo_ref, lam_ref, sub_ref, g_ref, b_ref,
                 o_ref, acc_ref, oat_ref, qz_ref, s_ref, m_ref, l_ref, *, lambda_init):
    qi = pl.program_id(1)
    lv = lam_ref[...]
    lam = (jnp.exp(jnp.sum(lv[0:1] * lv[1:2], axis=-1, keepdims=True))
           - jnp.exp(jnp.sum(lv[2:3] * lv[3:4], axis=-1, keepdims=True)) + lambda_init)

    lane = lax.broadcasted_iota(jnp.int32, (ATT_TQ, HEAD_W), 1)
    key_idx = lax.broadcasted_iota(jnp.int32, (ATT_TK, ATT_TQ), 0)
    qry_idx = lax.broadcasted_iota(jnp.int32, (ATT_TK, ATT_TQ), 1)

    heads = [slice(hd * HEAD_W, (hd + 1) * HEAD_W) for hd in range(A_HEADS)]
    for hd, sl in enumerate(heads):
        qh = q_ref[:, sl]
        qz_ref[2 * hd] = jnp.where(lane < QK_DIM, qh, jnp.zeros_like(qh))
        qz_ref[2 * hd + 1] = jnp.where(lane >= QK_DIM, qh, jnp.zeros_like(qh))
    acc_ref[...] = jnp.zeros(acc_ref.shape, F32)
    m_ref[...] = jnp.full(m_ref.shape, -jnp.inf, F32)
    l_ref[...] = jnp.zeros(l_ref.shape, F32)

    last_block = jnp.maximum(qi - 1, 0)

    def scores(buf, j, i):
        r = pl.multiple_of(j * ATT_TK, ATT_TK)
        kb = k_ref[pl.ds(r, ATT_TK), heads[i // 2]]
        s_ref[buf, i] = lax.dot_general(kb, qz_ref[i], NT_DIMS, preferred_element_type=F32)

    def consume(buf, j, i, masked):
        sc = s_ref[buf, i]
        if masked:
            sc = jnp.where(key_idx <= qry_idx, sc, -jnp.inf)
        m_old = m_ref[i:i + 1, :]
        m_new = jnp.maximum(m_old, jnp.max(sc, axis=0, keepdims=True))
        scale = jnp.exp2(m_old - m_new)
        p = jnp.exp2(sc - m_new)
        l_ref[i:i + 1, :] = scale * l_ref[i:i + 1, :] + jnp.sum(p, axis=0, keepdims=True)
        m_ref[i:i + 1, :] = m_new
        r = pl.multiple_of(j * ATT_TK, ATT_TK)
        vtb = vt_ref[heads[i // 2], pl.ds(r, ATT_TK)]
        acc_ref[i] = scale * acc_ref[i] + jnp.dot(vtb, p.astype(BF16), preferred_element_type=F32)

    def visit(buf, j, masked, prefetch):
        for i in range(N_MAPS):
            if prefetch is not None:
                scores(1 - buf, prefetch, i)
            consume(buf, j, i, masked)

    def pair(t, carry):
        visit(1, 2 * t, False, jnp.minimum(2 * t + 1, last_block))
        visit(0, 2 * t + 1, False, jnp.minimum(2 * t + 2, last_block))
        return carry

    scores(0, qi, 0)
    for i in range(N_MAPS):
        if i + 1 < N_MAPS:
            scores(0, qi, i + 1)
        scores(1, 0, i)
        consume(0, qi, i, True)
    lax.fori_loop(0, qi // 2, pair, 0)

    @pl.when(qi % 2 == 1)
    def _():
        visit(1, qi - 1, False, None)

    for hd, sl in enumerate(heads):
        o = (acc_ref[2 * hd] / l_ref[2 * hd:2 * hd + 1, :]
             - lam * (acc_ref[2 * hd + 1] / l_ref[2 * hd + 1:2 * hd + 2, :]))
        o = o * lax.rsqrt(jnp.mean(o * o, axis=0, keepdims=True) + LN_EPS) * sub_ref[...]
        oat_ref[sl, :] = (o * (1.0 - lambda_init)).astype(BF16)

    y = lax.dot_general(oat_ref[...], wo_ref[0:A_WIDTH, :], TN_DIMS, preferred_element_type=F32)
    y = y + jnp.dot(ob_ref[...], wo_ref[A_WIDTH:, :], preferred_element_type=F32)
    o_ref[...] = _layer_norm(ALPHA * x_ref[...] + y, g_ref[...], b_ref[...])


def _attn_out(qku, vt, x, w_o, layer, lam_vecs, subln, g, b, batch, lambda_init):
    t = x.shape[0]
    seq = t // batch
    nq = seq // ATT_TQ
    return pl.pallas_call(
        functools.partial(_attn_kernel, lambda_init=lambda_init),
        grid=(batch, nq),
        in_specs=[
            pl.BlockSpec((ATT_TQ, A_WIDTH), lambda bb, i: (bb * nq + i, 0)),
            pl.BlockSpec((seq, A_WIDTH), lambda bb, i: (bb, 1), pipeline_mode=pl.Buffered(1)),
            pl.BlockSpec((None, A_WIDTH, seq), lambda bb, i: (bb, 0, 0), pipeline_mode=pl.Buffered(1)),
            pl.BlockSpec((ATT_TQ, B_WIDTH), lambda bb, i: (bb * nq + i, 2)),
            pl.BlockSpec((ATT_TQ, D_MODEL), lambda bb, i: (bb * nq + i, 0)),
            _layer_slab((D_MODEL, D_MODEL), layer),
            _resident((4, QK_DIM)),
            _resident((HEAD_W, 1)),
            _resident((1, D_MODEL)),
            _resident((1, D_MODEL)),
        ],
        out_specs=pl.BlockSpec((ATT_TQ, D_MODEL), lambda bb, i: (bb * nq + i, 0)),
        out_shape=jax.ShapeDtypeStruct((t, D_MODEL), F32),
        scratch_shapes=[
            pltpu.VMEM((N_MAPS, HEAD_W, ATT_TQ), F32),
            pltpu.VMEM((A_WIDTH, ATT_TQ), BF16),
            pltpu.VMEM((N_MAPS, ATT_TQ, HEAD_W), BF16),
            pltpu.VMEM((2, N_MAPS, ATT_TK, ATT_TQ), F32),
            pltpu.VMEM((N_MAPS, ATT_TQ), F32),
            pltpu.VMEM((N_MAPS, ATT_TQ), F32),
        ],
        compiler_params=_compiler_params(("arbitrary", "arbitrary")),
        name="diff_attn_out",
    )(qku, qku, vt, qku, x, w_o, lam_vecs, subln, g, b)


C_TM = 512
C_PIECE = 256
N_PROJ = C_IN // D_MODEL
LOG_DECAY_TERMS = 2
N_SUB = CHUNK // SUB
HALF = SUB // 2
MASKED_EXPONENT = -1e30


def _hgrn_kernel(x_ref, xn_ref, win_ref, wo_ref, gn_ref, gamma_ref, g_ref, b_ref, o_ref,
                 h0_ref, h_ref, state_ref, obuf_ref, q_ref, b2_ref, c_ref, wd_ref,
                 qt_ref, kbar_ref, qhat_ref, khat_ref, vb_ref, *, layer):
    @pl.when(pl.program_id(1) == 0)
    def _():
        state_ref[...] = jnp.zeros(state_ref.shape, F32)

    gam = gamma_ref[...]
    e = jnp.exp(gam - jnp.max(gam, axis=0, keepdims=True))
    p = e / jnp.sum(e, axis=0, keepdims=True)
    lb = jnp.sum(p[1:layer + 1], axis=0, keepdims=True)

    def h_rows(piece, lo, hi):
        if piece == 0:
            return h0_ref, slice(lo, hi)
        base = (piece - 1) * C_PIECE
        return h_ref, slice(base + lo, base + hi)

    def project(piece, group, next_tile=False):
        cols = slice(group * D_MODEL, (group + 1) * D_MODEL)
        src = xn_ref[...] if next_tile else x_ref[piece * C_PIECE:(piece + 1) * C_PIECE, :]
        dst_ref, rows = h_rows(piece, 0, C_PIECE)
        dst_ref[rows, cols] = jnp.dot(src.astype(BF16), win_ref[:, cols], preferred_element_type=F32)

    chunks_per_piece = C_PIECE // CHUNK
    tri = (lax.broadcasted_iota(jnp.int32, (CHUNK, CHUNK), 1)
           <= lax.broadcasted_iota(jnp.int32, (CHUNK, CHUNK), 0)).astype(BF16)
    row8 = lax.broadcasted_iota(jnp.int32, (HALF, 1), 0)
    lane_s = lax.broadcasted_iota(jnp.int32, (HALF, CHUNK), 1)

    def chunk(c, prefetch):
        rows_c = slice(c * CHUNK, (c + 1) * CHUNK)
        lo = (c % chunks_per_piece) * CHUNK
        hc_ref, hrows = h_rows(c // chunks_per_piece, lo, lo + CHUNK)
        hq = hc_ref[hrows, 0:D_MODEL]
        q = hq * _sigmoid(hq)
        f = lb + (1.0 - lb) * _sigmoid(hc_ref[hrows, D_MODEL:2 * D_MODEL])
        kk = 1.0 - f
        iv = hc_ref[hrows, 2 * D_MODEL:3 * D_MODEL]
        log2f = jnp.log2(f)
        b2 = jnp.zeros((CHUNK, D_MODEL), F32)
        for _ in range(LOG_DECAY_TERMS):
            term = log2f.astype(BF16)
            b2 = b2 + jnp.dot(tri, term, preferred_element_type=F32)
            log2f = log2f - term.astype(F32)
        b2_last = b2[CHUNK - 1:CHUNK, :]
        q_ref[...] = q
        b2_ref[...] = b2
        c_ref[...] = b2 - jnp.log2(kk)
        qt_ref[...] = (q * jnp.exp2(b2)).astype(BF16)
        kbar_ref[...] = (kk * jnp.exp2(b2_last - b2)).astype(BF16)
        vb_ref[...] = iv.astype(BF16)

        for i in range(1, N_SUB):
            rows = slice(i * SUB, (i + 1) * SUB)
            ref_i = b2[i * SUB - 1:i * SUB, :]
            qhat_ref[rows, :] = (q[rows] * jnp.exp2(b2[rows] - ref_i)).astype(BF16)
            prev = slice(0, i * SUB)
            khat_ref[i - 1, prev, :] = (kk[prev] * jnp.exp2(ref_i - b2[prev])).astype(BF16)
            khat_ref[i - 1, i * SUB:, :] = jnp.zeros((CHUNK - i * SUB, D_MODEL), BF16)

        for hd in range(C_HEADS):
            hl = slice(hd * C_DIM, (hd + 1) * C_DIM)
            st = state_ref[hd * C_DIM:(hd + 1) * C_DIM, :]
            obuf_ref[rows_c, hl] = lax.dot_general(
                qt_ref[:, hl], st.astype(BF16), NT_DIMS, preferred_element_type=F32)
            wd_ref[hd, 0:SUB, :] = jnp.zeros((SUB, CHUNK), F32)
            for i in range(1, N_SUB):
                wd_ref[hd, i * SUB:(i + 1) * SUB, :] = lax.dot_general(
                    qhat_ref[i * SUB:(i + 1) * SUB, hl], khat_ref[i - 1, :, hl],
                    NT_DIMS, preferred_element_type=F32)
            upd = lax.dot_general(vb_ref[:, hl], kbar_ref[:, hl], TN_DIMS, preferred_element_type=F32)
            state_ref[hd * C_DIM:(hd + 1) * C_DIM, :] = st * jnp.exp2(b2_last[:, hl]) + upd

        for piece, group, next_tile in prefetch:
            project(piece, group, next_tile)

        for i in range(N_SUB):
            a = i * SUB
            q_half = (q_ref[a:a + HALF, :], q_ref[a + HALF:a + SUB, :])
            b_half = (b2_ref[a:a + HALF, :], b2_ref[a + HALF:a + SUB, :])
            w = [[jnp.zeros((HALF, CHUNK), F32) for _ in range(C_HEADS)] for _ in range(2)]
            for s in range(SUB):
                c_row = jnp.broadcast_to(c_ref[a + s:a + s + 1, :], (HALF, D_MODEL))
                for half in range(2):
                    first_row = s - half * HALF
                    if first_row >= HALF:
                        continue
                    arg = b_half[half] - c_row
                    if first_row > 0:
                        arg = jnp.where(row8 >= first_row, arg, MASKED_EXPONENT)
                    e_ts = q_half[half] * jnp.exp2(arg)
                    for hd in range(C_HEADS):
                        w_ts = jnp.sum(e_ts[:, hd * C_DIM:(hd + 1) * C_DIM], axis=-1, keepdims=True)
                        w[half][hd] = jnp.where(lane_s == a + s, w_ts, w[half][hd])
            for hd in range(C_HEADS):
                wd_ref[hd, a:a + HALF, :] += w[0][hd]
                wd_ref[hd, a + HALF:a + SUB, :] += w[1][hd]

        for hd in range(C_HEADS):
            hl = slice(hd * C_DIM, (hd + 1) * C_DIM)
            intra = jnp.dot(wd_ref[hd].astype(BF16), vb_ref[:, hl], preferred_element_type=F32)
            obuf_ref[rows_c, hl] += intra

    def finish(piece):
        rows = slice(piece * C_PIECE, (piece + 1) * C_PIECE)
        hp_ref, hrows = h_rows(piece, 0, C_PIECE)
        gz = hp_ref[hrows, 3 * D_MODEL:]
        gate = gz * _sigmoid(gz)
        for hd in range(C_HEADS):
            hl = slice(hd * C_DIM, (hd + 1) * C_DIM)
            oh = obuf_ref[rows, hl]
            oh = oh * lax.rsqrt(jnp.mean(oh * oh, axis=-1, keepdims=True) + LN_EPS) * gn_ref[...]
            obuf_ref[rows, hl] = oh * gate[:, hl]
        y = jnp.dot(obuf_ref[rows, :].astype(BF16), wo_ref[...], preferred_element_type=F32)
        o_ref[rows, :] = _layer_norm(ALPHA * x_ref[rows, :] + y, g_ref[...], b_ref[...])

    @pl.when((pl.program_id(0) == 0) & (pl.program_id(1) == 0))
    def _():
        for group in range(N_PROJ):
            project(0, group)

    n_pieces = C_TM // C_PIECE
    groups_per_chunk = N_PROJ // chunks_per_piece
    for piece in range(n_pieces):
        for cc in range(chunks_per_piece):
            groups = [cc * groups_per_chunk + g for g in range(groups_per_chunk)]
            if piece + 1 < n_pieces:
                prefetch = [(piece + 1, g, False) for g in groups]
            else:
                prefetch = [(0, g, True) for g in groups]
            chunk(piece * chunks_per_piece + cc, prefetch)
        finish(piece)


def _hgrn_layer(x, w_in, w_o, slab, g_norm, gamma, g, b, batch, layer):
    t = x.shape[0]
    nt = t // batch // C_TM
    pieces_per_tile = C_TM // C_PIECE

    def next_first_piece(bb, i):
        return (jnp.minimum((bb * nt + i + 1) * pieces_per_tile, t // C_PIECE - 1), 0)

    return pl.pallas_call(
        functools.partial(_hgrn_kernel, layer=layer),
        grid=(batch, nt),
        in_specs=[
            pl.BlockSpec((C_TM, D_MODEL), lambda bb, i: (bb * nt + i, 0)),
            pl.BlockSpec((C_PIECE, D_MODEL), next_first_piece),
            _layer_slab((D_MODEL, C_IN), slab),
            _layer_slab((D_MODEL, D_MODEL), slab),
            _resident((1, C_DIM)),
            _resident((DEPTH, D_MODEL)),
            _resident((1, D_MODEL)),
            _resident((1, D_MODEL)),
        ],
        out_specs=pl.BlockSpec((C_TM, D_MODEL), lambda bb, i: (bb * nt + i, 0)),
        out_shape=jax.ShapeDtypeStruct((t, D_MODEL), F32),
        scratch_shapes=[
            pltpu.VMEM((C_PIECE, C_IN), F32),
            pltpu.VMEM((C_TM - C_PIECE, C_IN), F32),
            pltpu.VMEM((C_HEADS * C_DIM, C_DIM), F32),
            pltpu.VMEM((C_TM, D_MODEL), F32),
            pltpu.VMEM((CHUNK, D_MODEL), F32),
            pltpu.VMEM((CHUNK, D_MODEL), F32),
            pltpu.VMEM((CHUNK, D_MODEL), F32),
            pltpu.VMEM((C_HEADS, CHUNK, CHUNK), F32),
            pltpu.VMEM((CHUNK, D_MODEL), BF16),
            pltpu.VMEM((CHUNK, D_MODEL), BF16),
            pltpu.VMEM((CHUNK, D_MODEL), BF16),
            pltpu.VMEM((N_SUB - 1, CHUNK, D_MODEL), BF16),
            pltpu.VMEM((CHUNK, D_MODEL), BF16),
        ],
        compiler_params=_compiler_params(("arbitrary", "arbitrary")),
        name="hgrn2_layer",
    )(x, x, w_in, w_o, g_norm, gamma, g, b)


def kernel(x, positions, ln_gain, ln_bias, ffn1_w_in, ffn1_w_out, ffn2_w_in, ffn2_w_out,
           ab_w_in, ab_w_o, diff_lambda, diff_subln, pool_w, pool_scale,
           c_w_in, c_w_o, c_norm, hgrn_gamma):
    batch, seq, d = x.shape
    t = batch * seq
    x = x.reshape(t, d)
    pos = positions.reshape(t, 1)
    inv_freq = ROPE_THETA ** (-jnp.arange(0, QK_DIM, 2, dtype=F32) / QK_DIM)
    freq = jnp.tile(inv_freq, HEAD_W // (QK_DIM // 2)).reshape(1, HEAD_W)

    def row(a):
        return a.reshape(1, -1)

    ab_out = ab_w_o.astype(BF16)
    ab_vt = jnp.swapaxes(ab_w_in[:, :, 2 * A_WIDTH:3 * A_WIDTH], 1, 2).astype(BF16)
    pool_wb = pool_w.astype(BF16)
    c_in, c_out = c_w_in.astype(BF16), c_w_o.astype(BF16)

    for l in range(DEPTH):
        j = l // 2
        x = _ffn_ln(x, ffn1_w_in, ffn1_w_out, l, row(ln_gain[l, 0]), row(ln_bias[l, 0]))
        if l % 2 == 0:
            lambda_init = 0.8 - 0.6 * math.exp(-0.3 * l)
            qku, vt = _ab_in(x, pos, ab_w_in, ab_vt, j, freq, pool_wb, row(pool_scale[j]), batch)
            x = _attn_out(qku, vt, x, ab_out, j, diff_lambda[j], diff_subln[j].reshape(-1, 1),
                          row(ln_gain[l, 1]), row(ln_bias[l, 1]), batch, lambda_init)
        else:
            x = _hgrn_layer(x, c_in, c_out, j, row(c_norm[j]),
                            hgrn_gamma, row(ln_gain[l, 1]), row(ln_bias[l, 1]), batch, l)
        x = _ffn_ln(x, ffn2_w_in, ffn2_w_out, l, row(ln_gain[l, 2]), row(ln_bias[l, 2]))
    return x.reshape(batch, seq, d)
```

```python
import functools
import math

import jax
import jax.numpy as jnp
from jax import lax
from jax.experimental import pallas as pl
from jax.experimental.pallas import tpu as pltpu

D_MODEL = 1024
DEPTH = 4
QK_DIM = 64
HEAD_W = 2 * QK_DIM
A_WIDTH = 512
A_HEADS = 4
ROPE_THETA = 10000.0
POOL_WINDOWS = (2, 4, 8, 16)
POOL_HALO = 16
B_WIDTH = 512
POOL_GROUP = 128
AB_IN = 3 * A_WIDTH + B_WIDTH
QKU_W = 2 * A_WIDTH + B_WIDTH
C_DIM = 128
C_HEADS = 8
C_IN = 4 * D_MODEL
CHUNK = 64
SUB = 16
D_FF = 2816
ALPHA = (2 * DEPTH) ** 0.25
LN_EPS = 1e-5

VMEM_LIMIT = 56 * 1024 * 1024

F32 = jnp.float32
BF16 = jnp.bfloat16
NT_DIMS = (((1,), (1,)), ((), ()))
TN_DIMS = (((0,), (0,)), ((), ()))


def _resident(shape):
    return pl.BlockSpec(shape, lambda *_: (0,) * len(shape), pipeline_mode=pl.Buffered(1))


def _layer_slab(shape, layer):
    return pl.BlockSpec((None,) + tuple(shape), lambda *_: (layer,) + (0,) * len(shape),
                        pipeline_mode=pl.Buffered(1))


def _compiler_params(semantics):
    return pltpu.CompilerParams(dimension_semantics=semantics, vmem_limit_bytes=VMEM_LIMIT)


def _layer_norm(z, g, b):
    mu = jnp.mean(z, axis=-1, keepdims=True)
    d = z - mu
    var = jnp.mean(d * d, axis=-1, keepdims=True)
    return d * lax.rsqrt(var + LN_EPS) * g + b


def _sigmoid(x):
    return 0.5 * jnp.tanh(0.5 * x) + 0.5


FFN_TM = 1024
FFN_SUB = 256
FFN_WSTEPS = 11
FFN_WIN_COLS = 2 * D_FF // FFN_WSTEPS
FFN_WOUT_ROWS = D_FF // FFN_WSTEPS


def _ffn_kernel(x_ref, win_ref, wout_ref, g_ref, b_ref, o_ref, winb_ref, woutb_ref, y_ref):
    step = pl.program_id(0)

    @pl.when(step < FFN_WSTEPS)
    def _():
        c0 = pl.multiple_of(step * FFN_WIN_COLS, FFN_WIN_COLS)
        winb_ref[:, pl.ds(c0, FFN_WIN_COLS)] = win_ref[...].astype(BF16)
        r0 = pl.multiple_of(step * FFN_WOUT_ROWS, FFN_WOUT_ROWS)
        woutb_ref[pl.ds(r0, FFN_WOUT_ROWS), :] = wout_ref[...].astype(BF16)

    @pl.when(step >= FFN_WSTEPS)
    def _():
        n_sub = FFN_TM // FFN_SUB
        for s in range(n_sub + 1):
            if s < n_sub:
                x = x_ref[s * FFN_SUB:(s + 1) * FFN_SUB, :]
                h = jnp.dot(x.astype(BF16), winb_ref[...], preferred_element_type=F32)
                gate = h[:, :D_FF]
                up = h[:, D_FF:]
                act = (gate * _sigmoid(gate) * up).astype(BF16)
                y = jnp.dot(act, woutb_ref[...], preferred_element_type=F32)
            if s > 0:
                prev = slice((s - 1) * FFN_SUB, s * FFN_SUB)
                o_ref[prev, :] = _layer_norm(ALPHA * x_ref[prev, :] + 0.5 * y_ref[...], g_ref[...], b_ref[...])
            if s < n_sub:
                y_ref[...] = y


def _ffn_ln(x, w_in, w_out, layer, g, b):
    t = x.shape[0]
    last = FFN_WSTEPS - 1

    def row_tile(i):
        return (jnp.maximum(i - FFN_WSTEPS, 0), 0)

    return pl.pallas_call(
        _ffn_kernel,
        grid=(FFN_WSTEPS + t // FFN_TM,),
        in_specs=[
            pl.BlockSpec((FFN_TM, D_MODEL), row_tile),
            pl.BlockSpec((None, D_MODEL, FFN_WIN_COLS), lambda i: (layer, 0, jnp.minimum(i, last))),
            pl.BlockSpec((None, FFN_WOUT_ROWS, D_MODEL), lambda i: (layer, jnp.minimum(i, last), 0)),
            _resident((1, D_MODEL)),
            _resident((1, D_MODEL)),
        ],
        out_specs=pl.BlockSpec((FFN_TM, D_MODEL), row_tile),
        out_shape=jax.ShapeDtypeStruct((t, D_MODEL), F32),
        scratch_shapes=[
            pltpu.VMEM((D_MODEL, 2 * D_FF), BF16),
            pltpu.VMEM((D_FF, D_MODEL), BF16),
            pltpu.VMEM((FFN_SUB, D_MODEL), F32),
        ],
        compiler_params=_compiler_params(("arbitrary",)),
        name="ffn_ln",
    )(x, w_in, w_out, g, b)


AB_TM = 1024
AB_SUB = 256
Q_SCALE = QK_DIM ** -0.5 * math.log2(math.e)


def _ab_in_kernel(x_ref, pos_ref, w_ref, wvt_ref, freq_ref, pw_ref, ps_ref, o_ref, vt_ref, ubuf_ref):
    i = pl.program_id(1)
    lane = lax.broadcasted_iota(jnp.int32, (AB_SUB, HEAD_W), 1)
    first_half = (lane % QK_DIM) < (QK_DIM // 2)

    @pl.when(i == 0)
    def _():
        ubuf_ref[0:POOL_HALO, :] = jnp.zeros((POOL_HALO, B_WIDTH), F32)

    for s in range(AB_TM // AB_SUB):
        rows = slice(s * AB_SUB, (s + 1) * AB_SUB)
        xb = x_ref[rows, :].astype(BF16)
        h = jnp.dot(xb, w_ref[:, 0:2 * A_WIDTH].astype(BF16), preferred_element_type=F32)
        u = jnp.dot(xb, w_ref[:, 3 * A_WIDTH:].astype(BF16), preferred_element_type=F32)
        vt_ref[:, rows] = lax.dot_general(wvt_ref[...], xb, NT_DIMS, preferred_element_type=F32).astype(BF16)

        ang = pos_ref[rows, :].astype(F32) * freq_ref[...]
        cos = jnp.cos(ang)
        sin = jnp.sin(ang)
        sin = jnp.where(first_half, -sin, sin)

        def rope(t):
            partner = jnp.where(first_half,
                                pltpu.roll(t, HEAD_W - QK_DIM // 2, axis=1),
                                pltpu.roll(t, QK_DIM // 2, axis=1))
            return t * cos + partner * sin

        for hd in range(A_HEADS):
            sl = slice(hd * HEAD_W, (hd + 1) * HEAD_W)
            q = rope(h[:, sl]) * Q_SCALE
            o_ref[rows, sl] = q.astype(BF16)
            ksl = slice(A_WIDTH + hd * HEAD_W, A_WIDTH + (hd + 1) * HEAD_W)
            o_ref[rows, ksl] = rope(h[:, ksl]).astype(BF16)

        ubuf_ref[POOL_HALO:, :] = u
        seq_idx = i * AB_TM + s * AB_SUB + lax.broadcasted_iota(jnp.int32, (AB_SUB, 1), 0)
        count = (seq_idx + 1).astype(F32)
        for g, w in enumerate(POOL_WINDOWS):
            sl = slice(g * POOL_GROUP, (g + 1) * POOL_GROUP)
            acc = ubuf_ref[:, sl]
            step = 1
            while step < w:
                acc = acc + pltpu.roll(acc, step, axis=0)
                step *= 2
            mean = acc[POOL_HALO:, :] / jnp.minimum(count, float(w))
            d = (mean - u[:, sl]).astype(BF16)
            y = jnp.dot(d, pw_ref[g], preferred_element_type=F32) * ps_ref[:, sl]
            o_ref[rows, 2 * A_WIDTH + g * POOL_GROUP:2 * A_WIDTH + (g + 1) * POOL_GROUP] = y.astype(BF16)
        ubuf_ref[0:POOL_HALO, :] = u[AB_SUB - POOL_HALO:, :]


def _ab_in(x, pos, w_in, w_vt, layer, freq, pool_w, pool_scale, batch):
    t = x.shape[0]
    seq = t // batch
    nt = seq // AB_TM
    return pl.pallas_call(
        _ab_in_kernel,
        grid=(batch, nt),
        in_specs=[
            pl.BlockSpec((AB_TM, D_MODEL), lambda b, i: (b * nt + i, 0)),
            pl.BlockSpec((AB_TM, 1), lambda b, i: (b * nt + i, 0)),
            _layer_slab((D_MODEL, AB_IN), layer),
            _layer_slab((A_WIDTH, D_MODEL), layer),
            _resident((1, HEAD_W)),
            _layer_slab((len(POOL_WINDOWS), POOL_GROUP, POOL_GROUP), layer),
            _resident((1, B_WIDTH)),
        ],
        out_specs=[
            pl.BlockSpec((AB_TM, QKU_W), lambda b, i: (b * nt + i, 0)),
            pl.BlockSpec((None, A_WIDTH, AB_TM), lambda b, i: (b, 0, i)),
        ],
        out_shape=[
            jax.ShapeDtypeStruct((t, QKU_W), BF16),
            jax.ShapeDtypeStruct((batch, A_WIDTH, seq), BF16),
        ],
        scratch_shapes=[pltpu.VMEM((POOL_HALO + AB_SUB, B_WIDTH), F32)],
        compiler_params=_compiler_params(("arbitrary", "arbitrary")),
        name="ab_in",
    )(x, pos, w_in, w_vt, freq, pool_w, pool_scale)


ATT_TQ = 512
ATT_TK = 512
N_MAPS = 2 * A_HEADS


def _attn_kernel(q_ref, k_ref, vt_ref, ob_ref, x_ref, wo_ref, lam_ref, sub_ref, g_ref, b_ref,
                 o_ref, acc_ref, oat_ref, qz_ref, s_ref, m_ref, l_ref, *, lambda_init):
    qi = pl.program_id(1)
    lv = lam_ref[...]
    lam = (jnp.exp(jnp.sum(lv[0:1] * lv[1:2], axis=-1, keepdims=True))
           - jnp.exp(jnp.sum(lv[2:3] * lv[3:4], axis=-1, keepdims=True)) + lambda_init)

    lane = lax.broadcasted_iota(jnp.int32, (ATT_TQ, HEAD_W), 1)
    key_idx = lax.broadcasted_iota(jnp.int32, (ATT_TK, ATT_TQ), 0)
    qry_idx = lax.broadcasted_iota(jnp.int32, (ATT_TK, ATT_TQ), 1)

    heads = [slice(hd * HEAD_W, (hd + 1) * HEAD_W) for hd in range(A_HEADS)]
    for hd, sl in enumerate(heads):
        qh = q_ref[:, sl]
        qz_ref[2 * hd] = jnp.where(lane < QK_DIM, qh, jnp.zeros_like(qh))
        qz_ref[2 * hd + 1] = jnp.where(lane >= QK_DIM, qh, jnp.zeros_like(qh))
    acc_ref[...] = jnp.zeros(acc_ref.shape, F32)
    m_ref[...] = jnp.full(m_ref.shape, -jnp.inf, F32)
    l_ref[...] = jnp.zeros(l_ref.shape, F32)

    last_block = jnp.maximum(qi - 1, 0)

    def scores(buf, j, i):
        r = pl.multiple_of(j * ATT_TK, ATT_TK)
        kb = k_ref[pl.ds(r, ATT_TK), heads[i // 2]]
        s_ref[buf, i] = lax.dot_general(kb, qz_ref[i], NT_DIMS, preferred_element_type=F32)

    def consume(buf, j, i, masked):
        sc = s_ref[buf, i]
        if masked:
            sc = jnp.where(key_idx <= qry_idx, sc, -jnp.inf)
        m_old = m_ref[i:i + 1, :]
        m_new = jnp.maximum(m_old, jnp.max(sc, axis=0, keepdims=True))
        scale = jnp.exp2(m_old - m_new)
        p = jnp.exp2(sc - m_new)
        l_ref[i:i + 1, :] = scale * l_ref[i:i + 1, :] + jnp.sum(p, axis=0, keepdims=True)
        m_ref[i:i + 1, :] = m_new
        r = pl.multiple_of(j * ATT_TK, ATT_TK)
        vtb = vt_ref[heads[i // 2], pl.ds(r, ATT_TK)]
        acc_ref[i] = scale * acc_ref[i] + jnp.dot(vtb, p.astype(BF16), preferred_element_type=F32)

    def visit(buf, j, masked, prefetch):
        for i in range(N_MAPS):
            if prefetch is not None:
                scores(1 - buf, prefetch, i)
            consume(buf, j, i, masked)

    def pair(t, carry):
        visit(1, 2 * t, False, jnp.minimum(2 * t + 1, last_block))
        visit(0, 2 * t + 1, False, jnp.minimum(2 * t + 2, last_block))
        return carry

    scores(0, qi, 0)
    for i in range(N_MAPS):
        if i + 1 < N_MAPS:
            scores(0, qi, i + 1)
        scores(1, 0, i)
        consume(0, qi, i, True)
    lax.fori_loop(0, qi // 2, pair, 0)

    @pl.when(qi % 2 == 1)
    def _():
        visit(1, qi - 1, False, None)

    for hd, sl in enumerate(heads):
        o = (acc_ref[2 * hd] / l_ref[2 * hd:2 * hd + 1, :]
             - lam * (acc_ref[2 * hd + 1] / l_ref[2 * hd + 1:2 * hd + 2, :]))
        o = o * lax.rsqrt(jnp.mean(o * o, axis=0, keepdims=True) + LN_EPS) * sub_ref[...]
        oat_ref[sl, :] = (o * (1.0 - lambda_init)).astype(BF16)

    y = lax.dot_general(oat_ref[...], wo_ref[0:A_WIDTH, :], TN_DIMS, preferred_element_type=F32)
    y = y + jnp.dot(ob_ref[...], wo_ref[A_WIDTH:, :], preferred_element_type=F32)
    o_ref[...] = _layer_norm(ALPHA * x_ref[...] + y, g_ref[...], b_ref[...])


def _attn_out(qku, vt, x, w_o, layer, lam_vecs, subln, g, b, batch, lambda_init):
    t = x.shape[0]
    seq = t // batch
    nq = seq // ATT_TQ
    return pl.pallas_call(
        functools.partial(_attn_kernel, lambda_init=lambda_init),
        grid=(batch, nq),
        in_specs=[
            pl.BlockSpec((ATT_TQ, A_WIDTH), lambda bb, i: (bb * nq + i, 0)),
            pl.BlockSpec((seq, A_WIDTH), lambda bb, i: (bb, 1), pipeline_mode=pl.Buffered(1)),
            pl.BlockSpec((None, A_WIDTH, seq), lambda bb, i: (bb, 0, 0), pipeline_mode=pl.Buffered(1)),
            pl.BlockSpec((ATT_TQ, B_WIDTH), lambda bb, i: (bb * nq + i, 2)),
            pl.BlockSpec((ATT_TQ, D_MODEL), lambda bb, i: (bb * nq + i, 0)),
            _layer_slab((D_MODEL, D_MODEL), layer),
            _resident((4, QK_DIM)),
            _resident((HEAD_W, 1)),
            _resident((1, D_MODEL)),
            _resident((1, D_MODEL)),
        ],
        out_specs=pl.BlockSpec((ATT_TQ, D_MODEL), lambda bb, i: (bb * nq + i, 0)),
        out_shape=jax.ShapeDtypeStruct((t, D_MODEL), F32),
        scratch_shapes=[
            pltpu.VMEM((N_MAPS, HEAD_W, ATT_TQ), F32),
            pltpu.VMEM((A_WIDTH, ATT_TQ), BF16),
            pltpu.VMEM((N_MAPS, ATT_TQ, HEAD_W), BF16),
            pltpu.VMEM((2, N_MAPS, ATT_TK, ATT_TQ), F32),
            pltpu.VMEM((N_MAPS, ATT_TQ), F32),
            pltpu.VMEM((N_MAPS, ATT_TQ), F32),
        ],
        compiler_params=_compiler_params(("arbitrary", "arbitrary")),
        name="diff_attn_out",
    )(qku, qku, vt, qku, x, w_o, lam_vecs, subln, g, b)


C_TM = 512
C_PIECE = 256
N_PROJ = C_IN // D_MODEL
LOG_DECAY_TERMS = 2
N_SUB = CHUNK // SUB
HALF = SUB // 2
MASKED_EXPONENT = -1e30


def _hgrn_kernel(x_ref, xn_ref, win_ref, wo_ref, gn_ref, gamma_ref, g_ref, b_ref, o_ref,
                 h0_ref, h_ref, state_ref, obuf_ref, q_ref, b2_ref, c_ref, wd_ref,
                 qt_ref, kbar_ref, qhat_ref, khat_ref, vb_ref, *, layer):
    @pl.when(pl.program_id(1) == 0)
    def _():
        state_ref[...] = jnp.zeros(state_ref.shape, F32)

    gam = gamma_ref[...]
    e = jnp.exp(gam - jnp.max(gam, axis=0, keepdims=True))
    p = e / jnp.sum(e, axis=0, keepdims=True)
    lb = jnp.sum(p[1:layer + 1], axis=0, keepdims=True)

    def h_rows(piece, lo, hi):
        if piece == 0:
            return h0_ref, slice(lo, hi)
        base = (piece - 1) * C_PIECE
        return h_ref, slice(base + lo, base + hi)

    def project(piece, group, next_tile=False):
        cols = slice(group * D_MODEL, (group + 1) * D_MODEL)
        src = xn_ref[...] if next_tile else x_ref[piece * C_PIECE:(piece + 1) * C_PIECE, :]
        dst_ref, rows = h_rows(piece, 0, C_PIECE)
        dst_ref[rows, cols] = jnp.dot(src.astype(BF16), win_ref[:, cols], preferred_element_type=F32)

    chunks_per_piece = C_PIECE // CHUNK
    tri = (lax.broadcasted_iota(jnp.int32, (CHUNK, CHUNK), 1)
           <= lax.broadcasted_iota(jnp.int32, (CHUNK, CHUNK), 0)).astype(BF16)
    row8 = lax.broadcasted_iota(jnp.int32, (HALF, 1), 0)
    lane_s = lax.broadcasted_iota(jnp.int32, (HALF, CHUNK), 1)

    def chunk(c, prefetch):
        rows_c = slice(c * CHUNK, (c + 1) * CHUNK)
        lo = (c % chunks_per_piece) * CHUNK
        hc_ref, hrows = h_rows(c // chunks_per_piece, lo, lo + CHUNK)
        hq = hc_ref[hrows, 0:D_MODEL]
        q = hq * _sigmoid(hq)
        f = lb + (1.0 - lb) * _sigmoid(hc_ref[hrows, D_MODEL:2 * D_MODEL])
        kk = 1.0 - f
        iv = hc_ref[hrows, 2 * D_MODEL:3 * D_MODEL]
        log2f = jnp.log2(f)
        b2 = jnp.zeros((CHUNK, D_MODEL), F32)
        for _ in range(LOG_DECAY_TERMS):
            term = log2f.astype(BF16)
            b2 = b2 + jnp.dot(tri, term, preferred_element_type=F32)
            log2f = log2f - term.astype(F32)
        b2_last = b2[CHUNK - 1:CHUNK, :]
        q_ref[...] = q
        b2_ref[...] = b2
        c_ref[...] = b2 - jnp.log2(kk)
        qt_ref[...] = (q * jnp.exp2(b2)).astype(BF16)
        kbar_ref[...] = (kk * jnp.exp2(b2_last - b2)).astype(BF16)
        vb_ref[...] = iv.astype(BF16)

        for i in range(1, N_SUB):
            rows = slice(i * SUB, (i + 1) * SUB)
            ref_i = b2[i * SUB - 1:i * SUB, :]
            qhat_ref[rows, :] = (q[rows] * jnp.exp2(b2[rows] - ref_i)).astype(BF16)
            prev = slice(0, i * SUB)
            khat_ref[i - 1, prev, :] = (kk[prev] * jnp.exp2(ref_i - b2[prev])).astype(BF16)
            khat_ref[i - 1, i * SUB:, :] = jnp.zeros((CHUNK - i * SUB, D_MODEL), BF16)

        for hd in range(C_HEADS):
            hl = slice(hd * C_DIM, (hd + 1) * C_DIM)
            st = state_ref[hd * C_DIM:(hd + 1) * C_DIM, :]
            obuf_ref[rows_c, hl] = lax.dot_general(
                qt_ref[:, hl], st.astype(BF16), NT_DIMS, preferred_element_type=F32)
            wd_ref[hd, 0:SUB, :] = jnp.zeros((SUB, CHUNK), F32)
            for i in range(1, N_SUB):
                wd_ref[hd, i * SUB:(i + 1) * SUB, :] = lax.dot_general(
                    qhat_ref[i * SUB:(i + 1) * SUB, hl], khat_ref[i - 1, :, hl],
                    NT_DIMS, preferred_element_type=F32)
            upd = lax.dot_general(vb_ref[:, hl], kbar_ref[:, hl], TN_DIMS, preferred_element_type=F32)
            state_ref[hd * C_DIM:(hd + 1) * C_DIM, :] = st * jnp.exp2(b2_last[:, hl]) + upd

        for piece, group, next_tile in prefetch:
            project(piece, group, next_tile)

        for i in range(N_SUB):
            a = i * SUB
            q_half = (q_ref[a:a + HALF, :], q_ref[a + HALF:a + SUB, :])
            b_half = (b2_ref[a:a + HALF, :], b2_ref[a + HALF:a + SUB, :])
            w = [[jnp.zeros((HALF, CHUNK), F32) for _ in range(C_HEADS)] for _ in range(2)]
            for s in range(SUB):
                c_row = jnp.broadcast_to(c_ref[a + s:a + s + 1, :], (HALF, D_MODEL))
                for half in range(2):
                    first_row = s - half * HALF
                    if first_row >= HALF:
                        continue
                    arg = b_half[half] - c_row
                    if first_row > 0:
                        arg = jnp.where(row8 >= first_row, arg, MASKED_EXPONENT)
                    e_ts = q_half[half] * jnp.exp2(arg)
                    for hd in range(C_HEADS):
                        w_ts = jnp.sum(e_ts[:, hd * C_DIM:(hd + 1) * C_DIM], axis=-1, keepdims=True)
                        w[half][hd] = jnp.where(lane_s == a + s, w_ts, w[half][hd])
            for hd in range(C_HEADS):
                wd_ref[hd, a:a + HALF, :] += w[0][hd]
                wd_ref[hd, a + HALF:a + SUB, :] += w[1][hd]

        for hd in range(C_HEADS):
            hl = slice(hd * C_DIM, (hd + 1) * C_DIM)
            intra = jnp.dot(wd_ref[hd].astype(BF16), vb_ref[:, hl], preferred_element_type=F32)
            obuf_ref[rows_c, hl] += intra

    def finish(piece):
        rows = slice(piece * C_PIECE, (piece + 1) * C_PIECE)
        hp_ref, hrows = h_rows(piece, 0, C_PIECE)
        gz = hp_ref[hrows, 3 * D_MODEL:]
        gate = gz * _sigmoid(gz)
        for hd in range(C_HEADS):
            hl = slice(hd * C_DIM, (hd + 1) * C_DIM)
            oh = obuf_ref[rows, hl]
            oh = oh * lax.rsqrt(jnp.mean(oh * oh, axis=-1, keepdims=True) + LN_EPS) * gn_ref[...]
            obuf_ref[rows, hl] = oh * gate[:, hl]
        y = jnp.dot(obuf_ref[rows, :].astype(BF16), wo_ref[...], preferred_element_type=F32)
        o_ref[rows, :] = _layer_norm(ALPHA * x_ref[rows, :] + y, g_ref[...], b_ref[...])

    @pl.when((pl.program_id(0) == 0) & (pl.program_id(1) == 0))
    def _():
        for group in range(N_PROJ):
            project(0, group)

    n_pieces = C_TM // C_PIECE
    groups_per_chunk = N_PROJ // chunks_per_piece
    for piece in range(n_pieces):
        for cc in range(chunks_per_piece):
            groups = [cc * groups_per_chunk + g for g in range(groups_per_chunk)]
            if piece + 1 < n_pieces:
                prefetch = [(piece + 1, g, False) for g in groups]
            else:
                prefetch = [(0, g, True) for g in groups]
            chunk(piece * chunks_per_piece + cc, prefetch)
        finish(piece)


def _hgrn_layer(x, w_in, w_o, slab, g_norm, gamma, g, b, batch, layer):
    t = x.shape[0]
    nt = t // batch // C_TM
    pieces_per_tile = C_TM // C_PIECE

    def next_first_piece(bb, i):
        return (jnp.minimum((bb * nt + i + 1) * pieces_per_tile, t // C_PIECE - 1), 0)

    return pl.pallas_call(
        functools.partial(_hgrn_kernel, layer=layer),
        grid=(batch, nt),
        in_specs=[
            pl.BlockSpec((C_TM, D_MODEL), lambda bb, i: (bb * nt + i, 0)),
            pl.BlockSpec((C_PIECE, D_MODEL), next_first_piece),
            _layer_slab((D_MODEL, C_IN), slab),
            _layer_slab((D_MODEL, D_MODEL), slab),
            _resident((1, C_DIM)),
            _resident((DEPTH, D_MODEL)),
            _resident((1, D_MODEL)),
            _resident((1, D_MODEL)),
        ],
        out_specs=pl.BlockSpec((C_TM, D_MODEL), lambda bb, i: (bb * nt + i, 0)),
        out_shape=jax.ShapeDtypeStruct((t, D_MODEL), F32),
        scratch_shapes=[
            pltpu.VMEM((C_PIECE, C_IN), F32),
            pltpu.VMEM((C_TM - C_PIECE, C_IN), F32),
            pltpu.VMEM((C_HEADS * C_DIM, C_DIM), F32),
            pltpu.VMEM((C_TM, D_MODEL), F32),
            pltpu.VMEM((CHUNK, D_MODEL), F32),
            pltpu.VMEM((CHUNK, D_MODEL), F32),
            pltpu.VMEM((CHUNK, D_MODEL), F32),
            pltpu.VMEM((C_HEADS, CHUNK, CHUNK), F32),
            pltpu.VMEM((CHUNK, D_MODEL), BF16),
            pltpu.VMEM((CHUNK, D_MODEL), BF16),
            pltpu.VMEM((CHUNK, D_MODEL), BF16),
            pltpu.VMEM((N_SUB - 1, CHUNK, D_MODEL), BF16),
            pltpu.VMEM((CHUNK, D_MODEL), BF16),
        ],
        compiler_params=_compiler_params(("arbitrary", "arbitrary")),
        name="hgrn2_layer",
    )(x, x, w_in, w_o, g_norm, gamma, g, b)


def kernel(x, positions, ln_gain, ln_bias, ffn1_w_in, ffn1_w_out, ffn2_w_in, ffn2_w_out,
           ab_w_in, ab_w_o, diff_lambda, diff_subln, pool_w, pool_scale,
           c_w_in, c_w_o, c_norm, hgrn_gamma):
    batch, seq, d = x.shape
    t = batch * seq
    x = x.reshape(t, d)
    pos = positions.reshape(t, 1)
    inv_freq = ROPE_THETA ** (-jnp.arange(0, QK_DIM, 2, dtype=F32) / QK_DIM)
    freq = jnp.tile(inv_freq, HEAD_W // (QK_DIM // 2)).reshape(1, HEAD_W)

    def row(a):
        return a.reshape(1, -1)

    ab_out = ab_w_o.astype(BF16)
    ab_vt = jnp.swapaxes(ab_w_in[:, :, 2 * A_WIDTH:3 * A_WIDTH], 1, 2).astype(BF16)
    pool_wb = pool_w.astype(BF16)
    c_in, c_out = c_w_in.astype(BF16), c_w_o.astype(BF16)

    for l in range(DEPTH):
        j = l // 2
        x = _ffn_ln(x, ffn1_w_in, ffn1_w_out, l, row(ln_gain[l, 0]), row(ln_bias[l, 0]))
        if l % 2 == 0:
            lambda_init = 0.8 - 0.6 * math.exp(-0.3 * l)
            qku, vt = _ab_in(x, pos, ab_w_in, ab_vt, j, freq, pool_wb, row(pool_scale[j]), batch)
            x = _attn_out(qku, vt, x, ab_out, j, diff_lambda[j], diff_subln[j].reshape(-1, 1),
                          row(ln_gain[l, 1]), row(ln_bias[l, 1]), batch, lambda_init)
        else:
            x = _hgrn_layer(x, c_in, c_out, j, row(c_norm[j]),
                            hgrn_gamma, row(ln_gain[l, 1]), row(ln_bias[l, 1]), batch, l)
        x = _ffn_ln(x, ffn2_w_in, ffn2_w_out, l, row(ln_gain[l, 2]), row(ln_bias[l, 2]))
    return x.reshape(batch, seq, d)
```

```python
import functools
import math

import jax
import jax.numpy as jnp
from jax import lax
from jax.experimental import pallas as pl
from jax.experimental.pallas import tpu as pltpu

D_MODEL = 1024
DEPTH = 4
QK_DIM = 64
HEAD_W = 2 * QK_DIM
A_WIDTH = 512
A_HEADS = 4
ROPE_THETA = 10000.0
POOL_WINDOWS = (2, 4, 8, 16)
POOL_HALO = 16
B_WIDTH = 512
POOL_GROUP = 128
AB_IN = 3 * A_WIDTH + B_WIDTH
QKU_W = 2 * A_WIDTH + B_WIDTH
C_DIM = 128
C_HEADS = 8
C_IN = 4 * D_MODEL
CHUNK = 64
SUB = 16
D_FF = 2816
ALPHA = (2 * DEPTH) ** 0.25
LN_EPS = 1e-5

VMEM_LIMIT = 56 * 1024 * 1024

F32 = jnp.float32
BF16 = jnp.bfloat16
NT_DIMS = (((1,), (1,)), ((), ()))
TN_DIMS = (((0,), (0,)), ((), ()))


def _resident(shape):
    return pl.BlockSpec(shape, lambda *_: (0,) * len(shape), pipeline_mode=pl.Buffered(1))


def _layer_slab(shape, layer):
    return pl.BlockSpec((None,) + tuple(shape), lambda *_: (layer,) + (0,) * len(shape),
                        pipeline_mode=pl.Buffered(1))


def _compiler_params(semantics):
    return pltpu.CompilerParams(dimension_semantics=semantics, vmem_limit_bytes=VMEM_LIMIT)


def _layer_norm(z, g, b):
    mu = jnp.mean(z, axis=-1, keepdims=True)
    d = z - mu
    var = jnp.mean(d * d, axis=-1, keepdims=True)
    return d * lax.rsqrt(var + LN_EPS) * g + b


def _sigmoid(x):
    return 0.5 * jnp.tanh(0.5 * x) + 0.5


FFN_TM = 1024
FFN_SUB = 256
FFN_WSTEPS = 11
FFN_WIN_COLS = 2 * D_FF // FFN_WSTEPS
FFN_WOUT_ROWS = D_FF // FFN_WSTEPS


def _ffn_kernel(x_ref, win_ref, wout_ref, g_ref, b_ref, o_ref, winb_ref, woutb_ref, y_ref):
    step = pl.program_id(0)

    @pl.when(step < FFN_WSTEPS)
    def _():
        c0 = pl.multiple_of(step * FFN_WIN_COLS, FFN_WIN_COLS)
        winb_ref[:, pl.ds(c0, FFN_WIN_COLS)] = win_ref[...].astype(BF16)
        r0 = pl.multiple_of(step * FFN_WOUT_ROWS, FFN_WOUT_ROWS)
        woutb_ref[pl.ds(r0, FFN_WOUT_ROWS), :] = wout_ref[...].astype(BF16)

    @pl.when(step >= FFN_WSTEPS)
    def _():
        n_sub = FFN_TM // FFN_SUB
        for s in range(n_sub + 1):
            if s < n_sub:
                x = x_ref[s * FFN_SUB:(s + 1) * FFN_SUB, :]
                xb = x.astype(BF16)
                gate = jnp.dot(xb, winb_ref[:, :D_FF], preferred_element_type=F32)
                up = jnp.dot(xb, winb_ref[:, D_FF:], preferred_element_type=F32)
                act = (gate * _sigmoid(gate) * up).astype(BF16)
                y = jnp.dot(act, woutb_ref[...], preferred_element_type=F32)
            if s > 0:
                prev = slice((s - 1) * FFN_SUB, s * FFN_SUB)
                o_ref[prev, :] = _layer_norm(ALPHA * x_ref[prev, :] + 0.5 * y_ref[...], g_ref[...], b_ref[...])
            if s < n_sub:
                y_ref[...] = y


def _ffn_ln(x, w_in, w_out, layer, g, b):
    t = x.shape[0]
    last = FFN_WSTEPS - 1

    def row_tile(i):
        return (jnp.maximum(i - FFN_WSTEPS, 0), 0)

    return pl.pallas_call(
        _ffn_kernel,
        grid=(FFN_WSTEPS + t // FFN_TM,),
        in_specs=[
            pl.BlockSpec((FFN_TM, D_MODEL), row_tile),
            pl.BlockSpec((None, D_MODEL, FFN_WIN_COLS), lambda i: (layer, 0, jnp.minimum(i, last))),
            pl.BlockSpec((None, FFN_WOUT_ROWS, D_MODEL), lambda i: (layer, jnp.minimum(i, last), 0)),
            _resident((1, D_MODEL)),
            _resident((1, D_MODEL)),
        ],
        out_specs=pl.BlockSpec((FFN_TM, D_MODEL), row_tile),
        out_shape=jax.ShapeDtypeStruct((t, D_MODEL), F32),
        scratch_shapes=[
            pltpu.VMEM((D_MODEL, 2 * D_FF), BF16),
            pltpu.VMEM((D_FF, D_MODEL), BF16),
            pltpu.VMEM((FFN_SUB, D_MODEL), F32),
        ],
        compiler_params=_compiler_params(("arbitrary",)),
        name="ffn_ln",
    )(x, w_in, w_out, g, b)


AB_TM = 1024
AB_SUB = 256
Q_SCALE = QK_DIM ** -0.5 * math.log2(math.e)


def _ab_in_kernel(x_ref, pos_ref, w_ref, wvt_ref, freq_ref, pw_ref, ps_ref, o_ref, vt_ref, ubuf_ref):
    i = pl.program_id(1)
    lane = lax.broadcasted_iota(jnp.int32, (AB_SUB, HEAD_W), 1)
    first_half = (lane % QK_DIM) < (QK_DIM // 2)

    @pl.when(i == 0)
    def _():
        ubuf_ref[0:POOL_HALO, :] = jnp.zeros((POOL_HALO, B_WIDTH), F32)

    for s in range(AB_TM // AB_SUB):
        rows = slice(s * AB_SUB, (s + 1) * AB_SUB)
        xb = x_ref[rows, :].astype(BF16)
        h = jnp.dot(xb, w_ref[:, 0:2 * A_WIDTH].astype(BF16), preferred_element_type=F32)
        u = jnp.dot(xb, w_ref[:, 3 * A_WIDTH:].astype(BF16), preferred_element_type=F32)
        vt_ref[:, rows] = lax.dot_general(wvt_ref[...], xb, NT_DIMS, preferred_element_type=F32).astype(BF16)

        ang = pos_ref[rows, :].astype(F32) * freq_ref[...]
        cos = jnp.cos(ang)
        sin = jnp.sin(ang)
        sin = jnp.where(first_half, -sin, sin)

        def rope(t):
            partner = jnp.where(first_half,
                                pltpu.roll(t, HEAD_W - QK_DIM // 2, axis=1),
                                pltpu.roll(t, QK_DIM // 2, axis=1))
            return t * cos + partner * sin

        for hd in range(A_HEADS):
            sl = slice(hd * HEAD_W, (hd + 1) * HEAD_W)
            q = rope(h[:, sl]) * Q_SCALE
            o_ref[rows, sl] = q.astype(BF16)
            ksl = slice(A_WIDTH + hd * HEAD_W, A_WIDTH + (hd + 1) * HEAD_W)
            o_ref[rows, ksl] = rope(h[:, ksl]).astype(BF16)

        ubuf_ref[POOL_HALO:, :] = u
        seq_idx = i * AB_TM + s * AB_SUB + lax.broadcasted_iota(jnp.int32, (AB_SUB, 1), 0)
        count = (seq_idx + 1).astype(F32)
        for g, w in enumerate(POOL_WINDOWS):
            sl = slice(g * POOL_GROUP, (g + 1) * POOL_GROUP)
            acc = ubuf_ref[:, sl]
            step = 1
            while step < w:
                acc = acc + pltpu.roll(acc, step, axis=0)
                step *= 2
            mean = acc[POOL_HALO:, :] / jnp.minimum(count, float(w))
            d = (mean - u[:, sl]).astype(BF16)
            y = jnp.dot(d, pw_ref[g], preferred_element_type=F32) * ps_ref[:, sl]
            o_ref[rows, 2 * A_WIDTH + g * POOL_GROUP:2 * A_WIDTH + (g + 1) * POOL_GROUP] = y.astype(BF16)
        ubuf_ref[0:POOL_HALO, :] = u[AB_SUB - POOL_HALO:, :]


def _ab_in(x, pos, w_in, w_vt, layer, freq, pool_w, pool_scale, batch):
    t = x.shape[0]
    seq = t // batch
    nt = seq // AB_TM
    return pl.pallas_call(
        _ab_in_kernel,
        grid=(batch, nt),
        in_specs=[
            pl.BlockSpec((AB_TM, D_MODEL), lambda b, i: (b * nt + i, 0)),
            pl.BlockSpec((AB_TM, 1), lambda b, i: (b * nt + i, 0)),
            _layer_slab((D_MODEL, AB_IN), layer),
            _layer_slab((A_WIDTH, D_MODEL), layer),
            _resident((1, HEAD_W)),
            _layer_slab((len(POOL_WINDOWS), POOL_GROUP, POOL_GROUP), layer),
            _resident((1, B_WIDTH)),
        ],
        out_specs=[
            pl.BlockSpec((AB_TM, QKU_W), lambda b, i: (b * nt + i, 0)),
            pl.BlockSpec((None, A_WIDTH, AB_TM), lambda b, i: (b, 0, i)),
        ],
        out_shape=[
            jax.ShapeDtypeStruct((t, QKU_W), BF16),
            jax.ShapeDtypeStruct((batch, A_WIDTH, seq), BF16),
        ],
        scratch_shapes=[pltpu.VMEM((POOL_HALO + AB_SUB, B_WIDTH), F32)],
        compiler_params=_compiler_params(("arbitrary", "arbitrary")),
        name="ab_in",
    )(x, pos, w_in, w_vt, freq, pool_w, pool_scale)


ATT_TQ = 512
ATT_TK = 512
N_MAPS = 2 * A_HEADS


def _attn_kernel(q_ref, k_ref, vt_ref, ob_ref, x_ref, wo_ref, lam_ref, sub_ref, g_ref, b_ref,
                 o_ref, acc_ref, oat_ref, qz_ref, s_ref, m_ref, l_ref, *, lambda_init):
    qi = pl.program_id(1)
    lv = lam_ref[...]
    lam = (jnp.exp(jnp.sum(lv[0:1] * lv[1:2], axis=-1, keepdims=True))
           - jnp.exp(jnp.sum(lv[2:3] * lv[3:4], axis=-1, keepdims=True)) + lambda_init)

    lane = lax.broadcasted_iota(jnp.int32, (ATT_TQ, HEAD_W), 1)
    key_idx = lax.broadcasted_iota(jnp.int32, (ATT_TK, ATT_TQ), 0)
    qry_idx = lax.broadcasted_iota(jnp.int32, (ATT_TK, ATT_TQ), 1)

    heads = [slice(hd * HEAD_W, (hd + 1) * HEAD_W) for hd in range(A_HEADS)]
    for hd, sl in enumerate(heads):
        qh = q_ref[:, sl]
        qz_ref[2 * hd] = jnp.where(lane < QK_DIM, qh, jnp.zeros_like(qh))
        qz_ref[2 * hd + 1] = jnp.where(lane >= QK_DIM, qh, jnp.zeros_like(qh))
    acc_ref[...] = jnp.zeros(acc_ref.shape, F32)
    m_ref[...] = jnp.full(m_ref.shape, -jnp.inf, F32)
    l_ref[...] = jnp.zeros(l_ref.shape, F32)

    last_block = jnp.maximum(qi - 1, 0)

    def scores(buf, j, i):
        r = pl.multiple_of(j * ATT_TK, ATT_TK)
        kb = k_ref[pl.ds(r, ATT_TK), heads[i // 2]]
        s_ref[buf, i] = lax.dot_general(kb, qz_ref[i], NT_DIMS, preferred_element_type=F32)

    def consume(buf, j, i, masked):
        sc = s_ref[buf, i]
        if masked:
            sc = jnp.where(key_idx <= qry_idx, sc, -jnp.inf)
        m_old = m_ref[i:i + 1, :]
        m_new = jnp.maximum(m_old, jnp.max(sc, axis=0, keepdims=True))
        scale = jnp.exp2(m_old - m_new)
        p = jnp.exp2(sc - m_new)
        l_ref[i:i + 1, :] = scale * l_ref[i:i + 1, :] + jnp.sum(p, axis=0, keepdims=True)
        m_ref[i:i + 1, :] = m_new
        r = pl.multiple_of(j * ATT_TK, ATT_TK)
        vtb = vt_ref[heads[i // 2], pl.ds(r, ATT_TK)]
        acc_ref[i] = scale * acc_ref[i] + jnp.dot(vtb, p.astype(BF16), preferred_element_type=F32)

    def visit(buf, j, masked, prefetch):
        for i in range(N_MAPS):
            if prefetch is not None:
                scores(1 - buf, prefetch, i)
            consume(buf, j, i, masked)

    def pair(t, carry):
        visit(1, 2 * t, False, jnp.minimum(2 * t + 1, last_block))
        visit(0, 2 * t + 1, False, jnp.minimum(2 * t + 2, last_block))
        return carry

    scores(0, qi, 0)
    for i in range(N_MAPS):
        if i + 1 < N_MAPS:
            scores(0, qi, i + 1)
        scores(1, 0, i)
        consume(0, qi, i, True)
    lax.fori_loop(0, qi // 2, pair, 0)

    @pl.when(qi % 2 == 1)
    def _():
        visit(1, qi - 1, False, None)

    for hd, sl in enumerate(heads):
        o = (acc_ref[2 * hd] / l_ref[2 * hd:2 * hd + 1, :]
             - lam * (acc_ref[2 * hd + 1] / l_ref[2 * hd + 1:2 * hd + 2, :]))
        o = o * lax.rsqrt(jnp.mean(o * o, axis=0, keepdims=True) + LN_EPS) * sub_ref[...]
        oat_ref[sl, :] = (o * (1.0 - lambda_init)).astype(BF16)

    y = lax.dot_general(oat_ref[...], wo_ref[0:A_WIDTH, :], TN_DIMS, preferred_element_type=F32)
    y = y + jnp.dot(ob_ref[...], wo_ref[A_WIDTH:, :], preferred_element_type=F32)
    o_ref[...] = _layer_norm(ALPHA * x_ref[...] + y, g_ref[...], b_ref[...])


def _attn_out(qku, vt, x, w_o, layer, lam_vecs, subln, g, b, batch, lambda_init):
    t = x.shape[0]
    seq = t // batch
    nq = seq // ATT_TQ
    return pl.pallas_call(
        functools.partial(_attn_kernel, lambda_init=lambda_init),
        grid=(batch, nq),
        in_specs=[
            pl.BlockSpec((ATT_TQ, A_WIDTH), lambda bb, i: (bb * nq + i, 0)),
            pl.BlockSpec((seq, A_WIDTH), lambda bb, i: (bb, 1), pipeline_mode=pl.Buffered(1)),
            pl.BlockSpec((None, A_WIDTH, seq), lambda bb, i: (bb, 0, 0), pipeline_mode=pl.Buffered(1)),
            pl.BlockSpec((ATT_TQ, B_WIDTH), lambda bb, i: (bb * nq + i, 2)),
            pl.BlockSpec((ATT_TQ, D_MODEL), lambda bb, i: (bb * nq + i, 0)),
            _layer_slab((D_MODEL, D_MODEL), layer),
            _resident((4, QK_DIM)),
            _resident((HEAD_W, 1)),
            _resident((1, D_MODEL)),
            _resident((1, D_MODEL)),
        ],
        out_specs=pl.BlockSpec((ATT_TQ, D_MODEL), lambda bb, i: (bb * nq + i, 0)),
        out_shape=jax.ShapeDtypeStruct((t, D_MODEL), F32),
        scratch_shapes=[
            pltpu.VMEM((N_MAPS, HEAD_W, ATT_TQ), F32),
            pltpu.VMEM((A_WIDTH, ATT_TQ), BF16),
            pltpu.VMEM((N_MAPS, ATT_TQ, HEAD_W), BF16),
            pltpu.VMEM((2, N_MAPS, ATT_TK, ATT_TQ), F32),
            pltpu.VMEM((N_MAPS, ATT_TQ), F32),
            pltpu.VMEM((N_MAPS, ATT_TQ), F32),
        ],
        compiler_params=_compiler_params(("arbitrary", "arbitrary")),
        name="diff_attn_out",
    )(qku, qku, vt, qku, x, w_o, lam_vecs, subln, g, b)


C_TM = 512
C_PIECE = 256
N_PROJ = C_IN // D_MODEL
LOG_DECAY_TERMS = 2
N_SUB = CHUNK // SUB
HALF = SUB // 2
MASKED_EXPONENT = -1e30


def _hgrn_kernel(x_ref, xn_ref, win_ref, wo_ref, gn_ref, gamma_ref, g_ref, b_ref, o_ref,
                 h0_ref, h_ref, state_ref, obuf_ref, q_ref, b2_ref, c_ref, wd_ref,
                 qt_ref, kbar_ref, qhat_ref, khat_ref, vb_ref, *, layer):
    @pl.when(pl.program_id(1) == 0)
    def _():
        state_ref[...] = jnp.zeros(state_ref.shape, F32)

    gam = gamma_ref[...]
    e = jnp.exp(gam - jnp.max(gam, axis=0, keepdims=True))
    p = e / jnp.sum(e, axis=0, keepdims=True)
    lb = jnp.sum(p[1:layer + 1], axis=0, keepdims=True)

    def h_rows(piece, lo, hi):
        if piece == 0:
            return h0_ref, slice(lo, hi)
        base = (piece - 1) * C_PIECE
        return h_ref, slice(base + lo, base + hi)

    def project(piece, group, next_tile=False):
        cols = slice(group * D_MODEL, (group + 1) * D_MODEL)
        src = xn_ref[...] if next_tile else x_ref[piece * C_PIECE:(piece + 1) * C_PIECE, :]
        dst_ref, rows = h_rows(piece, 0, C_PIECE)
        dst_ref[rows, cols] = jnp.dot(src.astype(BF16), win_ref[:, cols], preferred_element_type=F32)

    chunks_per_piece = C_PIECE // CHUNK
    tri = (lax.broadcasted_iota(jnp.int32, (CHUNK, CHUNK), 1)
           <= lax.broadcasted_iota(jnp.int32, (CHUNK, CHUNK), 0)).astype(BF16)
    row8 = lax.broadcasted_iota(jnp.int32, (HALF, 1), 0)
    lane_s = lax.broadcasted_iota(jnp.int32, (HALF, CHUNK), 1)

    def chunk(c, prefetch):
        rows_c = slice(c * CHUNK, (c + 1) * CHUNK)
        lo = (c % chunks_per_piece) * CHUNK
        hc_ref, hrows = h_rows(c // chunks_per_piece, lo, lo + CHUNK)
        hq = hc_ref[hrows, 0:D_MODEL]
        q = hq * _sigmoid(hq)
        f = lb + (1.0 - lb) * _sigmoid(hc_ref[hrows, D_MODEL:2 * D_MODEL])
        kk = 1.0 - f
        iv = hc_ref[hrows, 2 * D_MODEL:3 * D_MODEL]
        log2f = jnp.log2(f)
        b2 = jnp.zeros((CHUNK, D_MODEL), F32)
        for _ in range(LOG_DECAY_TERMS):
            term = log2f.astype(BF16)
            b2 = b2 + jnp.dot(tri, term, preferred_element_type=F32)
            log2f = log2f - term.astype(F32)
        b2_last = b2[CHUNK - 1:CHUNK, :]
        q_ref[...] = q
        b2_ref[...] = b2
        c_ref[...] = b2 - jnp.log2(kk)
        qt_ref[...] = (q * jnp.exp2(b2)).astype(BF16)
        kbar_ref[...] = (kk * jnp.exp2(b2_last - b2)).astype(BF16)
        vb_ref[...] = iv.astype(BF16)

        for i in range(1, N_SUB):
            rows = slice(i * SUB, (i + 1) * SUB)
            ref_i = b2[i * SUB - 1:i * SUB, :]
            qhat_ref[rows, :] = (q[rows] * jnp.exp2(b2[rows] - ref_i)).astype(BF16)
            prev = slice(0, i * SUB)
            khat_ref[i - 1, prev, :] = (kk[prev] * jnp.exp2(ref_i - b2[prev])).astype(BF16)
            khat_ref[i - 1, i * SUB:, :] = jnp.zeros((CHUNK - i * SUB, D_MODEL), BF16)

        for hd in range(C_HEADS):
            hl = slice(hd * C_DIM, (hd + 1) * C_DIM)
            st = state_ref[hd * C_DIM:(hd + 1) * C_DIM, :]
            obuf_ref[rows_c, hl] = lax.dot_general(
                qt_ref[:, hl], st.astype(BF16), NT_DIMS, preferred_element_type=F32)
            wd_ref[hd, 0:SUB, :] = jnp.zeros((SUB, CHUNK), F32)
            for i in range(1, N_SUB):
                wd_ref[hd, i * SUB:(i + 1) * SUB, :] = lax.dot_general(
                    qhat_ref[i * SUB:(i + 1) * SUB, hl], khat_ref[i - 1, :, hl],
                    NT_DIMS, preferred_element_type=F32)
            upd = lax.dot_general(vb_ref[:, hl], kbar_ref[:, hl], TN_DIMS, preferred_element_type=F32)
            state_ref[hd * C_DIM:(hd + 1) * C_DIM, :] = st * jnp.exp2(b2_last[:, hl]) + upd

        for piece, group, next_tile in prefetch:
            project(piece, group, next_tile)

        for i in range(N_SUB):
            a = i * SUB
            q_half = (q_ref[a:a + HALF, :], q_ref[a + HALF:a + SUB, :])
            b_half = (b2_ref[a:a + HALF, :], b2_ref[a + HALF:a + SUB, :])
            w = [[jnp.zeros((HALF, CHUNK), F32) for _ in range(C_HEADS)] for _ in range(2)]
            for s in range(SUB):
                c_row = jnp.broadcast_to(c_ref[a + s:a + s + 1, :], (HALF, D_MODEL))
                for half in range(2):
                    first_row = s - half * HALF
                    if first_row >= HALF:
                        continue
                    arg = b_half[half] - c_row
                    if first_row > 0:
                        arg = jnp.where(row8 >= first_row, arg, MASKED_EXPONENT)
                    e_ts = q_half[half] * jnp.exp2(arg)
                    for hd in range(C_HEADS):
                        w_ts = jnp.sum(e_ts[:, hd * C_DIM:(hd + 1) * C_DIM], axis=-1, keepdims=True)
                        w[half][hd] = jnp.where(lane_s == a + s, w_ts, w[half][hd])
            for hd in range(C_HEADS):
                wd_ref[hd, a:a + HALF, :] += w[0][hd]
                wd_ref[hd, a + HALF:a + SUB, :] += w[1][hd]

        for hd in range(C_HEADS):
            hl = slice(hd * C_DIM, (hd + 1) * C_DIM)
            intra = jnp.dot(wd_ref[hd].astype(BF16), vb_ref[:, hl], preferred_element_type=F32)
            obuf_ref[rows_c, hl] += intra

    def finish(piece):
        rows = slice(piece * C_PIECE, (piece + 1) * C_PIECE)
        hp_ref, hrows = h_rows(piece, 0, C_PIECE)
        gz = hp_ref[hrows, 3 * D_MODEL:]
        gate = gz * _sigmoid(gz)
        for hd in range(C_HEADS):
            hl = slice(hd * C_DIM, (hd + 1) * C_DIM)
            oh = obuf_ref[rows, hl]
            oh = oh * lax.rsqrt(jnp.mean(oh * oh, axis=-1, keepdims=True) + LN_EPS) * gn_ref[...]
            obuf_ref[rows, hl] = oh * gate[:, hl]
        y = jnp.dot(obuf_ref[rows, :].astype(BF16), wo_ref[...], preferred_element_type=F32)
        o_ref[rows, :] = _layer_norm(ALPHA * x_ref[rows, :] + y, g_ref[...], b_ref[...])

    @pl.when((pl.program_id(0) == 0) & (pl.program_id(1) == 0))
    def _():
        for group in range(N_PROJ):
            project(0, group)

    n_pieces = C_TM // C_PIECE
    groups_per_chunk = N_PROJ // chunks_per_piece
    for piece in range(n_pieces):
        for cc in range(chunks_per_piece):
            groups = [cc * groups_per_chunk + g for g in range(groups_per_chunk)]
            if piece + 1 < n_pieces:
                prefetch = [(piece + 1, g, False) for g in groups]
            else:
                prefetch = [(0, g, True) for g in groups]
            chunk(piece * chunks_per_piece + cc, prefetch)
        finish(piece)


def _hgrn_layer(x, w_in, w_o, slab, g_norm, gamma, g, b, batch, layer):
    t = x.shape[0]
    nt = t // batch // C_TM
    pieces_per_tile = C_TM // C_PIECE

    def next_first_piece(bb, i):
        return (jnp.minimum((bb * nt + i + 1) * pieces_per_tile, t // C_PIECE - 1), 0)

    return pl.pallas_call(
        functools.partial(_hgrn_kernel, layer=layer),
        grid=(batch, nt),
        in_specs=[
            pl.BlockSpec((C_TM, D_MODEL), lambda bb, i: (bb * nt + i, 0)),
            pl.BlockSpec((C_PIECE, D_MODEL), next_first_piece),
            _layer_slab((D_MODEL, C_IN), slab),
            _layer_slab((D_MODEL, D_MODEL), slab),
            _resident((1, C_DIM)),
            _resident((DEPTH, D_MODEL)),
            _resident((1, D_MODEL)),
            _resident((1, D_MODEL)),
        ],
        out_specs=pl.BlockSpec((C_TM, D_MODEL), lambda bb, i: (bb * nt + i, 0)),
        out_shape=jax.ShapeDtypeStruct((t, D_MODEL), F32),
        scratch_shapes=[
            pltpu.VMEM((C_PIECE, C_IN), F32),
            pltpu.VMEM((C_TM - C_PIECE, C_IN), F32),
            pltpu.VMEM((C_HEADS * C_DIM, C_DIM), F32),
            pltpu.VMEM((C_TM, D_MODEL), F32),
            pltpu.VMEM((CHUNK, D_MODEL), F32),
            pltpu.VMEM((CHUNK, D_MODEL), F32),
            pltpu.VMEM((CHUNK, D_MODEL), F32),
            pltpu.VMEM((C_HEADS, CHUNK, CHUNK), F32),
            pltpu.VMEM((CHUNK, D_MODEL), BF16),
            pltpu.VMEM((CHUNK, D_MODEL), BF16),
            pltpu.VMEM((CHUNK, D_MODEL), BF16),
            pltpu.VMEM((N_SUB - 1, CHUNK, D_MODEL), BF16),
            pltpu.VMEM((CHUNK, D_MODEL), BF16),
        ],
        compiler_params=_compiler_params(("arbitrary", "arbitrary")),
        name="hgrn2_layer",
    )(x, x, w_in, w_o, g_norm, gamma, g, b)


def kernel(x, positions, ln_gain, ln_bias, ffn1_w_in, ffn1_w_out, ffn2_w_in, ffn2_w_out,
           ab_w_in, ab_w_o, diff_lambda, diff_subln, pool_w, pool_scale,
           c_w_in, c_w_o, c_norm, hgrn_gamma):
    batch, seq, d = x.shape
    t = batch * seq
    x = x.reshape(t, d)
    pos = positions.reshape(t, 1)
    inv_freq = ROPE_THETA ** (-jnp.arange(0, QK_DIM, 2, dtype=F32) / QK_DIM)
    freq = jnp.tile(inv_freq, HEAD_W // (QK_DIM // 2)).reshape(1, HEAD_W)

    def row(a):
        return a.reshape(1, -1)

    ab_out = ab_w_o.astype(BF16)
    ab_vt = jnp.swapaxes(ab_w_in[:, :, 2 * A_WIDTH:3 * A_WIDTH], 1, 2).astype(BF16)
    pool_wb = pool_w.astype(BF16)
    c_in, c_out = c_w_in.astype(BF16), c_w_o.astype(BF16)

    for l in range(DEPTH):
        j = l // 2
        x = _ffn_ln(x, ffn1_w_in, ffn1_w_out, l, row(ln_gain[l, 0]), row(ln_bias[l, 0]))
        if l % 2 == 0:
            lambda_init = 0.8 - 0.6 * math.exp(-0.3 * l)
            qku, vt = _ab_in(x, pos, ab_w_in, ab_vt, j, freq, pool_wb, row(pool_scale[j]), batch)
            x = _attn_out(qku, vt, x, ab_out, j, diff_lambda[j], diff_subln[j].reshape(-1, 1),
                          row(ln_gain[l, 1]), row(ln_bias[l, 1]), batch, lambda_init)
        else:
            x = _hgrn_layer(x, c_in, c_out, j, row(c_norm[j]),
                            hgrn_gamma, row(ln_gain[l, 1]), row(ln_bias[l, 1]), batch, l)
        x = _ffn_ln(x, ffn2_w_in, ffn2_w_out, l, row(ln_gain[l, 2]), row(ln_bias[l, 2]))
    return x.reshape(batch, seq, d)
```
